```python
import math
import jax, jax.numpy as jnp
from jax import lax
import numpy as np

D_MODEL = 1024
BATCH = 8
SEQ = 2048
DEPTH = 2

HEAD_DIM = 64
A_HEADS = D_MODEL // 4 // HEAD_DIM
A_WIDTH = A_HEADS * HEAD_DIM
B_HEADS = D_MODEL // 2 // (2 * HEAD_DIM)
B_QK = B_HEADS * 2 * HEAD_DIM
B_V = B_HEADS * 2 * HEAD_DIM
C_WIDTH = D_MODEL // 4
MIX_WIDTH = A_WIDTH + B_V + C_WIDTH
IN_COLS = 3 * A_WIDTH + 2 * B_QK + B_V + 3 * C_WIDTH
SPLITS = (A_WIDTH, 2 * A_WIDTH, 3 * A_WIDTH,
          3 * A_WIDTH + B_QK, 3 * A_WIDTH + 2 * B_QK, 3 * A_WIDTH + 2 * B_QK + B_V,
          3 * A_WIDTH + 2 * B_QK + B_V + C_WIDTH, 3 * A_WIDTH + 2 * B_QK + B_V + 2 * C_WIDTH)
DILATED_PATTERNS = ((128, 1), (512, 4), (2048, 16))
Q_BLOCK = 128
CONV_WIDTH = 3
D_FF = ((8 * D_MODEL // 3 + 127) // 128) * 128
ROPE_THETA = 500000.0
ROPE_FRACTION = 4
NORM_EPS = 1e-6
SUBLN_EPS = 1e-5

kernel_name = "hybrid_dilated_diffattn_shortconv_block"


def rms_norm(x, g, eps=NORM_EPS):
    xf = x.astype(jnp.float32)
    y = xf * lax.rsqrt(jnp.mean(xf * xf, axis=-1, keepdims=True) + eps)
    return (y * g.astype(jnp.float32)).astype(x.dtype)


def partial_rope(x, pos):
    e = x.shape[-1]
    r = e // ROPE_FRACTION
    half = r // 2
    inv = ROPE_THETA ** (-(jnp.arange(half, dtype=jnp.float32) * 2.0 / r))
    ang = pos.astype(jnp.float32)[..., None] * inv
    cos = jnp.cos(ang)[:, :, None, :]
    sin = jnp.sin(ang)[:, :, None, :]
    xr = x[..., :r].astype(jnp.float32)
    x1, x2 = xr[..., :half], xr[..., half:]
    rot = jnp.concatenate([x1 * cos - x2 * sin, x1 * sin + x2 * cos], axis=-1).astype(x.dtype)
    return jnp.concatenate([rot, x[..., r:]], axis=-1)


def causal_dwconv(x, w):
    c = x.shape[-1]
    return lax.conv_general_dilated(
        x, w.astype(x.dtype)[:, None, :], window_strides=(1,),
        padding=[(CONV_WIDTH - 1, 0)], dimension_numbers=('NWC', 'WIO', 'NWC'),
        feature_group_count=c)


def dilated_window_branch(q, k, v, window, dilation):
    b, s, h, e = q.shape
    steps = window // dilation
    L = s // dilation
    nblk = -(-L // steps)
    Lp = nblk * steps

    def to_blocks(t):
        t = t.reshape(b, L, dilation, h, t.shape[-1]).transpose(0, 2, 3, 1, 4)
        t = jnp.pad(t, ((0, 0), (0, 0), (0, 0), (0, Lp - L), (0, 0)))
        return t.reshape(b, dilation, h, nblk, steps, t.shape[-1])

    def with_prev(t):
        prev = jnp.pad(t, ((0, 0), (0, 0), (0, 0), (1, 0), (0, 0), (0, 0)))[:, :, :, :-1]
        return jnp.concatenate([prev, t], axis=4)

    qb = to_blocks(q)
    kc = with_prev(to_blocks(k))
    vc = with_prev(to_blocks(v))
    sc = jnp.einsum('brhnqe,brhnke->brhnqk', qb, kc, preferred_element_type=jnp.float32)
    qi = jnp.arange(steps)[:, None]
    kj = jnp.arange(2 * steps)[None, :]
    dist = qi + steps - kj
    key_idx = jnp.arange(nblk)[:, None, None] * steps + kj - steps
    valid = (dist >= 0) & (dist <= steps) & (key_idx >= 0)
    sc = jnp.where(valid, sc, -jnp.inf)
    m = jnp.max(sc, axis=-1, keepdims=True)
    p = jnp.exp(sc - m)
    l = jnp.sum(p, axis=-1, keepdims=True)
    o = jnp.einsum('brhnqk,brhnke->brhnqe', p, vc) / l

    def from_blocks(t):
        f = t.shape[-1]
        t = t.reshape(b, dilation, h, Lp, f)[:, :, :, :L]
        return t.transpose(0, 3, 1, 2, 4).reshape(b, s, h, f)

    return from_blocks(o), from_blocks(m), from_blocks(l)


def dilated_mixture_attention(q, k, v):
    outs = [dilated_window_branch(q, k, v, w, d) for (w, d) in DILATED_PATTERNS]
    m_all = outs[0][1]
    for _, m_i, _ in outs[1:]:
        m_all = jnp.maximum(m_all, m_i)
    num = 0.0
    den = 0.0
    for o_i, m_i, l_i in outs:
        w_i = l_i * jnp.exp(m_i - m_all)
        num = num + w_i * o_i
        den = den + w_i
    return num / den


def differential_attention(q1, q2, k1, k2, v, lam):
    b, s, h, e = q1.shape
    nqb = s // Q_BLOCK
    kpos = jnp.arange(s)

    def blocks(t):
        return t.reshape(b, nqb, Q_BLOCK, h, t.shape[-1]).transpose(1, 0, 2, 3, 4)

    def one_block(args):
        q1b, q2b, i = args
        qpos = i * Q_BLOCK + jnp.arange(Q_BLOCK)
        causal = kpos[None, :] <= qpos[:, None]

        def attn(qb, kk):
            sc = jnp.einsum('bqhe,bkhe->bhqk', qb, kk, preferred_element_type=jnp.float32)
            return jax.nn.softmax(jnp.where(causal, sc, -jnp.inf), axis=-1)

        a = attn(q1b, k1) - lam * attn(q2b, k2)
        return jnp.einsum('bhqk,bkhe->bqhe', a, v)

    o = lax.map(one_block, (blocks(q1), blocks(q2), jnp.arange(nqb)))
    return o.transpose(1, 0, 2, 3, 4).reshape(b, s, h, v.shape[-1])


def hybrid_mixer(h, pos, w_in, lq1, lk1, lq2, lk2, subln_g, conv_w, w_out, layer_idx):
    b, s, _ = h.shape
    proj = h @ w_in
    aq, ak, av, bq, bk, bv, cb, cc, ch = jnp.split(proj, SPLITS, axis=-1)
    scale = HEAD_DIM ** -0.5

    aq = partial_rope(aq.reshape(b, s, A_HEADS, HEAD_DIM), pos) * scale
    ak = partial_rope(ak.reshape(b, s, A_HEADS, HEAD_DIM), pos)
    av = av.reshape(b, s, A_HEADS, HEAD_DIM)
    o_a = dilated_mixture_attention(aq, ak, av).astype(h.dtype).reshape(b, s, A_WIDTH)

    bq = bq.reshape(b, s, B_HEADS, 2, HEAD_DIM)
    bk = bk.reshape(b, s, B_HEADS, 2, HEAD_DIM)
    q1 = partial_rope(bq[:, :, :, 0], pos) * scale
    q2 = partial_rope(bq[:, :, :, 1], pos) * scale
    k1 = partial_rope(bk[:, :, :, 0], pos)
    k2 = partial_rope(bk[:, :, :, 1], pos)
    bv = bv.reshape(b, s, B_HEADS, 2 * HEAD_DIM)
    lam_init = 0.8 - 0.6 * math.exp(-0.3 * layer_idx)
    lam = (jnp.exp(jnp.sum(lq1.astype(jnp.float32) * lk1.astype(jnp.float32)))
           - jnp.exp(jnp.sum(lq2.astype(jnp.float32) * lk2.astype(jnp.float32))) + lam_init)
    o_b = differential_attention(q1, q2, k1, k2, bv, lam)
    o_b = rms_norm(o_b, subln_g, SUBLN_EPS) * (1.0 - lam_init)
    o_b = o_b.astype(h.dtype).reshape(b, s, B_V)

    o_c = cb * causal_dwconv(cc * ch, conv_w)

    return jnp.concatenate([o_a, o_b, o_c], axis=-1) @ w_out


def conv_glu_ffn(h, w_up, conv_w, w_down):
    gu = h @ w_up
    g, u = jnp.split(gu, 2, axis=-1)
    g = causal_dwconv(g, conv_w)
    return (jax.nn.silu(g) * u) @ w_down


def setup_inputs(seed: int = 0) -> dict:
    key = jax.random.key(seed)
    ks = jax.random.split(key, 16)

    def nrm(k, shape, scale):
        return jax.random.normal(k, shape, jnp.float32) * scale

    return {
        "x": nrm(ks[0], (BATCH, SEQ, D_MODEL), 1.0),
        "positions": jnp.tile(jnp.arange(SEQ, dtype=jnp.int32)[None, :], (BATCH, 1)),
        "norm_mix_g": 1.0 + nrm(ks[1], (DEPTH, D_MODEL), 0.05),
        "w_in": nrm(ks[2], (DEPTH, D_MODEL, IN_COLS), D_MODEL ** -0.5),
        "lambda_q1": nrm(ks[3], (DEPTH, HEAD_DIM), 0.1),
        "lambda_k1": nrm(ks[4], (DEPTH, HEAD_DIM), 0.1),
        "lambda_q2": nrm(ks[5], (DEPTH, HEAD_DIM), 0.1),
        "lambda_k2": nrm(ks[6], (DEPTH, HEAD_DIM), 0.1),
        "subln_g": 1.0 + nrm(ks[7], (DEPTH, 2 * HEAD_DIM), 0.05),
        "conv_mix_w": nrm(ks[8], (DEPTH, CONV_WIDTH, C_WIDTH), CONV_WIDTH ** -0.5),
        "w_out": nrm(ks[9], (DEPTH, MIX_WIDTH, D_MODEL), MIX_WIDTH ** -0.5),
        "norm_ffn_g": 1.0 + nrm(ks[10], (DEPTH, D_MODEL), 0.05),
        "w_up": nrm(ks[11], (DEPTH, D_MODEL, 2 * D_FF), D_MODEL ** -0.5),
        "conv_ffn_w": nrm(ks[12], (DEPTH, CONV_WIDTH, D_FF), CONV_WIDTH ** -0.5),
        "w_down": nrm(ks[13], (DEPTH, D_FF, D_MODEL), D_FF ** -0.5),
        "final_g": 1.0 + nrm(ks[14], (D_MODEL,), 0.05),
    }


def reference(x, positions, norm_mix_g, w_in, lambda_q1, lambda_k1, lambda_q2, lambda_k2,
              subln_g, conv_mix_w, w_out, norm_ffn_g, w_up, conv_ffn_w, w_down, final_g):
    h = x
    for layer in range(DEPTH):
        hn = rms_norm(h, norm_mix_g[layer])
        h = h + hybrid_mixer(hn, positions, w_in[layer], lambda_q1[layer], lambda_k1[layer],
                             lambda_q2[layer], lambda_k2[layer], subln_g[layer],
                             conv_mix_w[layer], w_out[layer], layer)
        hn = rms_norm(h, norm_ffn_g[layer])
        h = h + conv_glu_ffn(hn, w_up[layer], conv_ffn_w[layer], w_down[layer])
    return rms_norm(h, final_g)
```

```python
import functools
import math

import numpy as np
import jax
import jax.numpy as jnp
from jax import lax
from jax.experimental import pallas as pl
from jax.experimental.pallas import tpu as pltpu

D_MODEL = 1024
HEAD_DIM = 64
A_HEADS = 4
A_WIDTH = 256
B_HEADS = 4
B_QK = 512
B_V = 512
C_WIDTH = 256
IN_COLS = 3072
DILATED_PATTERNS = ((128, 1), (512, 4), (2048, 16))
CONV_WIDTH = 3
D_FF = 2816
ROPE_THETA = 500000.0
ROPE_DIMS = HEAD_DIM // 4
NORM_EPS = 1e-6
SUBLN_EPS = 1e-5

LANES = 128
CONV_HALO_ROWS = 8
MASK_BIAS = -1e30
VMEM_LIMIT_BYTES = 56 * 1024 * 1024

F32 = jnp.float32
BF16 = jnp.bfloat16


def _rms_norm(x, g, eps):
    y = x * lax.rsqrt(jnp.mean(x * x, axis=-1, keepdims=True) + eps)
    return y * g


def _causal_conv3(p, prev, w):
    rows = lax.broadcasted_iota(jnp.int32, p.shape, 0)
    last = prev[CONV_HALO_ROWS - 1:CONV_HALO_ROWS, :]
    last2 = prev[CONV_HALO_ROWS - 2:CONV_HALO_ROWS - 1, :]
    p1 = jnp.where(rows == 0, last, pltpu.roll(p, 1, axis=0))
    p2 = jnp.where(rows == 0, last2, jnp.where(rows == 1, last, pltpu.roll(p, 2, axis=0)))
    return w[0:1, :] * p2 + w[1:2, :] * p1 + w[2:3, :] * p


def _rope_table_kernel(pos_ref, inv_ref, c_ref, s1_ref, s2_ref):
    ang = pos_ref[...].astype(F32) * inv_ref[...]
    lane = lax.broadcasted_iota(jnp.int32, ang.shape, 1) % HEAD_DIM
    c = jnp.cos(ang)
    s = jnp.sin(ang)
    half = ROPE_DIMS // 2
    c_ref[...] = jnp.where(lane < ROPE_DIMS, c, 1.0)
    s1_ref[...] = jnp.where((lane >= half) & (lane < ROPE_DIMS), s, 0.0)
    s2_ref[...] = jnp.where(lane < half, -s, 0.0)


def _rope_tables(pos_col):
    t = pos_col.shape[0]
    half = ROPE_DIMS // 2
    inv = ROPE_THETA ** (-(jnp.arange(half, dtype=F32) * 2.0 / ROPE_DIMS))
    lane = np.arange(LANES) % HEAD_DIM
    inv_lane = jnp.where(lane < ROPE_DIMS, inv[lane % half], 0.0).astype(F32)[None, :]
    rows = 1024
    out = jax.ShapeDtypeStruct((t, LANES), F32)
    return pl.pallas_call(
        _rope_table_kernel,
        grid=(t // rows,),
        in_specs=[pl.BlockSpec((rows, 1), lambda i: (i, 0)),
                  pl.BlockSpec((1, LANES), lambda i: (0, 0))],
        out_specs=[pl.BlockSpec((rows, LANES), lambda i: (i, 0))] * 3,
        out_shape=[out] * 3,
        name="rope_tables",
    )(pos_col, inv_lane)


IN_ROWS = 512


def _in_proj_kernel(h_ref, g_ref, w_ref, c_ref, s1_ref, s2_ref, cw_ref,
                    a_ref, b_ref, oc_ref, hn_ref, carry_ref, *, tiles_per_seq):
    i = pl.program_id(0)
    hn_ref[...] = _rms_norm(h_ref[...], g_ref[...], NORM_EPS).astype(BF16)

    cos = c_ref[...]
    sin_up = s1_ref[...]
    sin_dn = s2_ref[...]
    half = ROPE_DIMS // 2

    def proj(c0, width):
        return jnp.dot(hn_ref[...], w_ref[:, c0:c0 + width], preferred_element_type=F32)

    def rope(y):
        return (y * cos + pltpu.roll(y, half, axis=1) * sin_up
                + pltpu.roll(y, LANES - half, axis=1) * sin_dn)

    def rope_cols(c0, width, scale, out_ref, o0):
        y = proj(c0, width)
        for k in range(width // LANES):
            r = rope(y[:, k * LANES:(k + 1) * LANES])
            if scale != 1.0:
                r = r * scale
            out_ref[:, o0 + k * LANES:o0 + (k + 1) * LANES] = r.astype(BF16)

    scale = HEAD_DIM ** -0.5
    rope_cols(0, A_WIDTH, scale, a_ref, 0)
    rope_cols(A_WIDTH, A_WIDTH, 1.0, a_ref, A_WIDTH)
    a_ref[:, 2 * A_WIDTH:3 * A_WIDTH] = proj(2 * A_WIDTH, A_WIDTH).astype(BF16)
    b0 = 3 * A_WIDTH
    rope_cols(b0, B_QK, scale, b_ref, 0)
    rope_cols(b0 + B_QK, B_QK, 1.0, b_ref, B_QK)
    b_ref[:, 2 * B_QK:2 * B_QK + B_V] = proj(b0 + 2 * B_QK, B_V).astype(BF16)

    c0 = b0 + 2 * B_QK + B_V
    gate = proj(c0, C_WIDTH)
    prod = proj(c0 + C_WIDTH, C_WIDTH) * proj(c0 + 2 * C_WIDTH, C_WIDTH)

    @pl.when(i % tiles_per_seq == 0)
    def _():
        carry_ref[...] = jnp.zeros_like(carry_ref)

    prev = carry_ref[...]
    carry_ref[...] = prod[prod.shape[0] - CONV_HALO_ROWS:, :]
    oc_ref[...] = (gate * _causal_conv3(prod, prev, cw_ref[...])).astype(BF16)


def _in_proj(h, g, w, tables, conv_w, seq):
    t = h.shape[0]
    rows = IN_ROWS
    cos, s1, s2 = tables
    row_spec = lambda width: pl.BlockSpec((rows, width), lambda i: (i, 0))
    full = lambda shape: pl.BlockSpec(shape, lambda i: (0, 0))
    return pl.pallas_call(
        functools.partial(_in_proj_kernel, tiles_per_seq=seq // rows),
        grid=(t // rows,),
        in_specs=[row_spec(D_MODEL), full((1, D_MODEL)), full((D_MODEL, IN_COLS)),
                  row_spec(LANES), row_spec(LANES), row_spec(LANES), full((CONV_WIDTH, C_WIDTH))],
        out_specs=[row_spec(3 * A_WIDTH), row_spec(2 * B_QK + B_V), row_spec(C_WIDTH)],
        out_shape=[jax.ShapeDtypeStruct((t, 3 * A_WIDTH), BF16),
                   jax.ShapeDtypeStruct((t, 2 * B_QK + B_V), BF16),
                   jax.ShapeDtypeStruct((t, C_WIDTH), BF16)],
        scratch_shapes=[pltpu.VMEM((rows, D_MODEL), BF16),
                        pltpu.VMEM((CONV_HALO_ROWS, C_WIDTH), F32)],
        compiler_params=pltpu.CompilerParams(dimension_semantics=("arbitrary",),
                                             vmem_limit_bytes=VMEM_LIMIT_BYTES),
        name="in_proj",
    )(h, g, w, cos, s1, s2, conv_w)


ATT_BLOCK = 256


def _dilated_bias_table(seq):
    blk = ATT_BLOCK
    r = np.arange(blk)[:, None]
    c = np.arange(blk)[None, :]
    out = np.empty((seq // blk, blk, blk), np.float32)
    for d in range(seq // blk):
        delta = d * blk + r - c
        count = np.zeros((blk, blk), np.int64)
        for window, dil in DILATED_PATTERNS:
            count += (delta >= 0) & (delta <= window) & (delta % dil == 0)
        out[d] = np.where(count > 0, np.log(np.maximum(count, 1)), MASK_BIAS)
    return out


def _causal_bias_table():
    blk = ATT_BLOCK
    r = np.arange(blk)[:, None]
    c = np.arange(blk)[None, :]
    return np.where(c <= r, 0.0, MASK_BIAS).astype(np.float32)[None]


def _pair_attention(q_ref, k_ref, v_ref, bias_ref, m_ref, l_ref, acc_ref, *, bias_all_blocks):
    blk = ATT_BLOCK
    i = pl.program_id(2)
    q = q_ref[...]
    lane = lax.broadcasted_iota(jnp.int32, q.shape, 1)
    zero = jnp.zeros_like(q)
    qs = jnp.concatenate([jnp.where(lane < HEAD_DIM, q, zero),
                          jnp.where(lane >= HEAD_DIM, q, zero)], axis=0)

    m_ref[...] = jnp.full_like(m_ref, -jnp.inf)
    l_ref[...] = jnp.zeros_like(l_ref)
    acc_ref[...] = jnp.zeros_like(acc_ref)

    def step(j, bias):
        start = pl.multiple_of(j * blk, blk)
        k = k_ref[pl.ds(start, blk), :]
        v = v_ref[pl.ds(start, blk), :]
        s = lax.dot_general(qs, k, (((1,), (1,)), ((), ())), preferred_element_type=F32)
        if bias is not None:
            s = s + jnp.concatenate([bias, bias], axis=0)
        m_prev = m_ref[...]
        m_new = jnp.maximum(m_prev, jnp.max(s, axis=1, keepdims=True))
        alpha = jnp.exp(m_prev - m_new)
        p = jnp.exp(s - m_new)
        l_ref[...] = alpha * l_ref[...] + jnp.sum(p, axis=1, keepdims=True)
        acc_ref[...] = alpha * acc_ref[...] + jnp.dot(p.astype(BF16), v, preferred_element_type=F32)
        m_ref[...] = m_new

    def off_diagonal(j, carry):
        step(j, bias_ref[i - j] if bias_all_blocks else None)
        return carry

    lax.fori_loop(0, i, off_diagonal, 0)
    step(i, bias_ref[0])
    return acc_ref[...], l_ref[...]


def _dilated_attn_kernel(q_ref, k_ref, v_ref, bias_ref, o_ref, m_ref, l_ref, acc_ref):
    blk = ATT_BLOCK
    acc, l = _pair_attention(q_ref, k_ref, v_ref, bias_ref, m_ref, l_ref, acc_ref,
                             bias_all_blocks=True)
    o = acc / l
    lane = lax.broadcasted_iota(jnp.int32, (blk, LANES), 1)
    o_ref[...] = jnp.where(lane < HEAD_DIM, o[:blk], o[blk:]).astype(o_ref.dtype)


def _diff_attn_kernel(q_ref, k_ref, v_ref, bias_ref, lq1_ref, lk1_ref, lq2_ref, lk2_ref, g_ref,
                      o_ref, m_ref, l_ref, acc_ref, *, lam_init):
    blk = ATT_BLOCK
    acc, l = _pair_attention(q_ref, k_ref, v_ref, bias_ref, m_ref, l_ref, acc_ref,
                             bias_all_blocks=False)
    lam = (jnp.exp(jnp.sum(lq1_ref[...] * lk1_ref[...], axis=1, keepdims=True))
           - jnp.exp(jnp.sum(lq2_ref[...] * lk2_ref[...], axis=1, keepdims=True)) + lam_init)
    o = acc / l
    o = o[:blk] - lam * o[blk:]
    o = _rms_norm(o, g_ref[...], SUBLN_EPS) * (1.0 - lam_init)
    o_ref[...] = o.astype(o_ref.dtype)


def _attention_call(body, qkv, bias, extra, seq, n_pairs, q_off, k_off, v_off, name):
    t = qkv.shape[0]
    blk = ATT_BLOCK
    nq = seq // blk
    batch = t // seq
    small = lambda a: pl.BlockSpec(a.shape, lambda b, p, i: (0,) * a.ndim)
    return pl.pallas_call(
        body,
        grid=(batch, n_pairs, nq),
        in_specs=[pl.BlockSpec((blk, LANES), lambda b, p, i: (b * nq + i, q_off + p)),
                  pl.BlockSpec((seq, LANES), lambda b, p, i: (b, k_off + p)),
                  pl.BlockSpec((seq, LANES), lambda b, p, i: (b, v_off + p)),
                  small(bias)] + [small(a) for a in extra],
        out_specs=pl.BlockSpec((blk, LANES), lambda b, p, i: (b * nq + i, p)),
        out_shape=jax.ShapeDtypeStruct((t, n_pairs * LANES), BF16),
        scratch_shapes=[pltpu.VMEM((2 * blk, 1), F32), pltpu.VMEM((2 * blk, 1), F32),
                        pltpu.VMEM((2 * blk, LANES), F32)],
        compiler_params=pltpu.CompilerParams(
            dimension_semantics=("arbitrary", "arbitrary", "arbitrary"),
            vmem_limit_bytes=VMEM_LIMIT_BYTES),
        name=name,
    )(qkv, qkv, qkv, bias, *extra)


OUT_ROWS = 512


def _out_proj_kernel(h_ref, oa_ref, ob_ref, oc_ref, w_ref, o_ref):
    mix = jnp.dot(oa_ref[...], w_ref[0:A_WIDTH, :], preferred_element_type=F32)
    mix += jnp.dot(ob_ref[...], w_ref[A_WIDTH:A_WIDTH + B_V, :], preferred_element_type=F32)
    mix += jnp.dot(oc_ref[...], w_ref[A_WIDTH + B_V:, :], preferred_element_type=F32)
    o_ref[...] = h_ref[...] + mix


def _out_proj(h, o_a, o_b, o_c, w):
    t = h.shape[0]
    rows = OUT_ROWS
    row_spec = lambda width: pl.BlockSpec((rows, width), lambda i: (i, 0))
    return pl.pallas_call(
        _out_proj_kernel,
        grid=(t // rows,),
        in_specs=[row_spec(D_MODEL), row_spec(A_WIDTH), row_spec(B_V), row_spec(C_WIDTH),
                  pl.BlockSpec(w.shape, lambda i: (0, 0))],
        out_specs=row_spec(D_MODEL),
        out_shape=jax.ShapeDtypeStruct((t, D_MODEL), F32),
        compiler_params=pltpu.CompilerParams(dimension_semantics=("arbitrary",),
                                             vmem_limit_bytes=VMEM_LIMIT_BYTES),
        name="out_proj",
    )(h, o_a, o_b, o_c, w)


FFN_ROWS = 512
FFN_COLS = D_FF // 2


def _ffn_kernel(h_ref, g_ref, wg_ref, wu_ref, cw_ref, wd_ref, fg_ref, o_ref,
                hn_ref, acc_ref, carry_ref, *, tiles_per_seq, final_norm):
    i = pl.program_id(0)
    j = pl.program_id(1)

    @pl.when(j == 0)
    def _():
        hn_ref[...] = _rms_norm(h_ref[...], g_ref[...], NORM_EPS).astype(BF16)
        acc_ref[...] = jnp.zeros_like(acc_ref)

    @pl.when(i % tiles_per_seq == 0)
    def _():
        carry_ref[j] = jnp.zeros(carry_ref.shape[1:], F32)

    hn = hn_ref[...]
    gate = jnp.dot(hn, wg_ref[...], preferred_element_type=F32)
    up = jnp.dot(hn, wu_ref[...], preferred_element_type=F32)
    prev = carry_ref[j]
    carry_ref[j] = gate[gate.shape[0] - CONV_HALO_ROWS:, :]
    gc = _causal_conv3(gate, prev, cw_ref[...])
    act = gc / (1.0 + jnp.exp(-gc)) * up
    acc_ref[...] += jnp.dot(act.astype(BF16), wd_ref[...], preferred_element_type=F32)

    @pl.when(j == pl.num_programs(1) - 1)
    def _():
        out = h_ref[...] + acc_ref[...]
        if final_norm:
            out = _rms_norm(out, fg_ref[...], NORM_EPS)
        o_ref[...] = out


def _ffn(h, g, w_up, conv_w, w_down, final_g, seq, final_norm):
    t = h.shape[0]
    rows, cols = FFN_ROWS, FFN_COLS
    nj = D_FF // cols
    return pl.pallas_call(
        functools.partial(_ffn_kernel, tiles_per_seq=seq // rows, final_norm=final_norm),
        grid=(t // rows, nj),
        in_specs=[pl.BlockSpec((rows, D_MODEL), lambda i, j: (i, 0)),
                  pl.BlockSpec((1, D_MODEL), lambda i, j: (0, 0)),
                  pl.BlockSpec((D_MODEL, cols), lambda i, j: (0, j)),
                  pl.BlockSpec((D_MODEL, cols), lambda i, j: (0, nj + j)),
                  pl.BlockSpec((CONV_WIDTH, cols), lambda i, j: (0, j)),
                  pl.BlockSpec((cols, D_MODEL), lambda i, j: (j, 0)),
                  pl.BlockSpec((1, D_MODEL), lambda i, j: (0, 0))],
        out_specs=pl.BlockSpec((rows, D_MODEL), lambda i, j: (i, 0)),
        out_shape=jax.ShapeDtypeStruct((t, D_MODEL), F32),
        scratch_shapes=[pltpu.VMEM((rows, D_MODEL), BF16),
                        pltpu.VMEM((rows, D_MODEL), F32),
                        pltpu.VMEM((nj, CONV_HALO_ROWS, cols), F32)],
        compiler_params=pltpu.CompilerParams(dimension_semantics=("arbitrary", "arbitrary"),
                                             vmem_limit_bytes=VMEM_LIMIT_BYTES),
        name="conv_glu_ffn",
    )(h, g, w_up, w_up, conv_w, w_down, final_g)


def kernel(x, positions, norm_mix_g, w_in, lambda_q1, lambda_k1, lambda_q2, lambda_k2, subln_g,
           conv_mix_w, w_out, norm_ffn_g, w_up, conv_ffn_w, w_down, final_g):
    batch, seq, d = x.shape
    depth = w_in.shape[0]
    assert d == D_MODEL and seq % ATT_BLOCK == 0 and seq % IN_ROWS == 0 and seq % FFN_ROWS == 0
    t = batch * seq
    h = x.reshape(t, d)
    tables = _rope_tables(positions.reshape(t, 1))
    bias_a = jnp.asarray(_dilated_bias_table(seq))
    bias_b = jnp.asarray(_causal_bias_table())
    row = lambda v: v.reshape(1, -1)

    for layer in range(depth):
        a_qkv, b_qkv, o_c = _in_proj(h, row(norm_mix_g[layer]), w_in[layer].astype(BF16), tables,
                                     conv_mix_w[layer], seq)
        o_a = _attention_call(_dilated_attn_kernel, a_qkv, bias_a, [], seq,
                              n_pairs=A_HEADS // 2, q_off=0, k_off=2, v_off=4, name="dilated_attn")
        lam_init = 0.8 - 0.6 * math.exp(-0.3 * layer)
        o_b = _attention_call(
            functools.partial(_diff_attn_kernel, lam_init=lam_init), b_qkv, bias_b,
            [row(lambda_q1[layer]), row(lambda_k1[layer]), row(lambda_q2[layer]),
             row(lambda_k2[layer]), row(subln_g[layer])],
            seq, n_pairs=B_HEADS, q_off=0, k_off=B_HEADS, v_off=2 * B_HEADS, name="diff_attn")
        h = _out_proj(h, o_a, o_b, o_c, w_out[layer].astype(BF16))
        h = _ffn(h, row(norm_ffn_g[layer]), w_up[layer].astype(BF16), conv_ffn_w[layer],
                 w_down[layer].astype(BF16), row(final_g), seq, final_norm=(layer == depth - 1))
    return h.reshape(batch, seq, d)
```

```python
import functools
import math

import numpy as np
import jax
import jax.numpy as jnp
from jax import lax
from jax.experimental import pallas as pl
from jax.experimental.pallas import tpu as pltpu

D_MODEL = 1024
HEAD_DIM = 64
A_HEADS = 4
A_WIDTH = 256
B_HEADS = 4
B_QK = 512
B_V = 512
C_WIDTH = 256
IN_COLS = 3072
DILATED_PATTERNS = ((128, 1), (512, 4), (2048, 16))
CONV_WIDTH = 3
D_FF = 2816
ROPE_THETA = 500000.0
ROPE_DIMS = HEAD_DIM // 4
NORM_EPS = 1e-6
SUBLN_EPS = 1e-5

LANES = 128
CONV_HALO_ROWS = 8
MASK_BIAS = -1e30
VMEM_LIMIT_BYTES = 56 * 1024 * 1024

F32 = jnp.float32
BF16 = jnp.bfloat16


def _rms_norm(x, g, eps):
    y = x * lax.rsqrt(jnp.mean(x * x, axis=-1, keepdims=True) + eps)
    return y * g


def _causal_conv3(p, prev, w):
    rows = lax.broadcasted_iota(jnp.int32, p.shape, 0)
    last = prev[CONV_HALO_ROWS - 1:CONV_HALO_ROWS, :]
    last2 = prev[CONV_HALO_ROWS - 2:CONV_HALO_ROWS - 1, :]
    p1 = jnp.where(rows == 0, last, pltpu.roll(p, 1, axis=0))
    p2 = jnp.where(rows == 0, last2, jnp.where(rows == 1, last, pltpu.roll(p, 2, axis=0)))
    return w[0:1, :] * p2 + w[1:2, :] * p1 + w[2:3, :] * p


def _rope_table_kernel(pos_ref, inv_ref, c_ref, s1_ref, s2_ref):
    ang = pos_ref[...].astype(F32) * inv_ref[...]
    lane = lax.broadcasted_iota(jnp.int32, ang.shape, 1) % HEAD_DIM
    c = jnp.cos(ang)
    s = jnp.sin(ang)
    half = ROPE_DIMS // 2
    c_ref[...] = jnp.where(lane < ROPE_DIMS, c, 1.0)
    s1_ref[...] = jnp.where((lane >= half) & (lane < ROPE_DIMS), s, 0.0)
    s2_ref[...] = jnp.where(lane < half, -s, 0.0)


def _rope_tables(pos_col):
    t = pos_col.shape[0]
    half = ROPE_DIMS // 2
    inv = ROPE_THETA ** (-(jnp.arange(half, dtype=F32) * 2.0 / ROPE_DIMS))
    lane = np.arange(LANES) % HEAD_DIM
    inv_lane = jnp.where(lane < ROPE_DIMS, inv[lane % half], 0.0).astype(F32)[None, :]
    rows = 1024
    out = jax.ShapeDtypeStruct((t, LANES), F32)
    return pl.pallas_call(
        _rope_table_kernel,
        grid=(t // rows,),
        in_specs=[pl.BlockSpec((rows, 1), lambda i: (i, 0)),
                  pl.BlockSpec((1, LANES), lambda i: (0, 0))],
        out_specs=[pl.BlockSpec((rows, LANES), lambda i: (i, 0))] * 3,
        out_shape=[out] * 3,
        name="rope_tables",
    )(pos_col, inv_lane)


IN_ROWS = 512


def _in_proj_kernel(h_ref, g_ref, w_ref, c_ref, s1_ref, s2_ref, cw_ref,
                    a_ref, b_ref, oc_ref, hn_ref, carry_ref, *, tiles_per_seq):
    i = pl.program_id(0)
    hn_ref[...] = _rms_norm(h_ref[...], g_ref[...], NORM_EPS).astype(BF16)

    cos = c_ref[...]
    sin_up = s1_ref[...]
    sin_dn = s2_ref[...]
    half = ROPE_DIMS // 2

    def proj(c0, width):
        return jnp.dot(hn_ref[...], w_ref[:, c0:c0 + width], preferred_element_type=F32)

    def rope(y):
        return (y * cos + pltpu.roll(y, half, axis=1) * sin_up
                + pltpu.roll(y, LANES - half, axis=1) * sin_dn)

    def rope_cols(c0, width, scale, out_ref, o0):
        y = proj(c0, width)
        for k in range(width // LANES):
            r = rope(y[:, k * LANES:(k + 1) * LANES])
            if scale != 1.0:
                r = r * scale
            out_ref[:, o0 + k * LANES:o0 + (k + 1) * LANES] = r.astype(BF16)

    scale = HEAD_DIM ** -0.5
    rope_cols(0, A_WIDTH, scale, a_ref, 0)
    rope_cols(A_WIDTH, A_WIDTH, 1.0, a_ref, A_WIDTH)
    a_ref[:, 2 * A_WIDTH:3 * A_WIDTH] = proj(2 * A_WIDTH, A_WIDTH).astype(BF16)
    b0 = 3 * A_WIDTH
    rope_cols(b0, B_QK, scale, b_ref, 0)
    rope_cols(b0 + B_QK, B_QK, 1.0, b_ref, B_QK)
    b_ref[:, 2 * B_QK:2 * B_QK + B_V] = proj(b0 + 2 * B_QK, B_V).astype(BF16)

    c0 = b0 + 2 * B_QK + B_V
    gate = proj(c0, C_WIDTH)
    prod = proj(c0 + C_WIDTH, C_WIDTH) * proj(c0 + 2 * C_WIDTH, C_WIDTH)

    @pl.when(i % tiles_per_seq == 0)
    def _():
        carry_ref[...] = jnp.zeros_like(carry_ref)

    prev = carry_ref[...]
    carry_ref[...] = prod[prod.shape[0] - CONV_HALO_ROWS:, :]
    oc_ref[...] = (gate * _causal_conv3(prod, prev, cw_ref[...])).astype(BF16)


def _in_proj(h, g, w, tables, conv_w, seq):
    t = h.shape[0]
    rows = IN_ROWS
    cos, s1, s2 = tables
    row_spec = lambda width: pl.BlockSpec((rows, width), lambda i: (i, 0))
    full = lambda shape: pl.BlockSpec(shape, lambda i: (0, 0))
    return pl.pallas_call(
        functools.partial(_in_proj_kernel, tiles_per_seq=seq // rows),
        grid=(t // rows,),
        in_specs=[row_spec(D_MODEL), full((1, D_MODEL)), full((D_MODEL, IN_COLS)),
                  row_spec(LANES), row_spec(LANES), row_spec(LANES), full((CONV_WIDTH, C_WIDTH))],
        out_specs=[row_spec(3 * A_WIDTH), row_spec(2 * B_QK + B_V), row_spec(C_WIDTH)],
        out_shape=[jax.ShapeDtypeStruct((t, 3 * A_WIDTH), BF16),
                   jax.ShapeDtypeStruct((t, 2 * B_QK + B_V), BF16),
                   jax.ShapeDtypeStruct((t, C_WIDTH), BF16)],
        scratch_shapes=[pltpu.VMEM((rows, D_MODEL), BF16),
                        pltpu.VMEM((CONV_HALO_ROWS, C_WIDTH), F32)],
        compiler_params=pltpu.CompilerParams(dimension_semantics=("arbitrary",),
                                             vmem_limit_bytes=VMEM_LIMIT_BYTES),
        name="in_proj",
    )(h, g, w, cos, s1, s2, conv_w)


ATT_ROWS = 256


def _dilated_bias_table(seq):
    r = np.arange(ATT_ROWS)[:, None]
    u = np.arange(seq)[None, :]
    delta = r + seq - ATT_ROWS - u
    count = np.zeros(delta.shape, np.int64)
    for window, dil in DILATED_PATTERNS:
        count += (delta >= 0) & (delta <= window) & (delta % dil == 0)
    return np.where(count > 0, np.log(np.maximum(count, 1)), MASK_BIAS).astype(np.float32)


def _causal_bias_table():
    r = np.arange(ATT_ROWS)[:, None]
    c = np.arange(ATT_ROWS)[None, :]
    return np.where(c <= r, 0.0, MASK_BIAS).astype(np.float32)


def _pair_attention(q_ref, k_ref, v_ref, add_bias, emit):
    rows = ATT_ROWS
    lane = lax.broadcasted_iota(jnp.int32, (rows, LANES), 1)
    for c in range(q_ref.shape[0] // rows):
        keys = (c + 1) * rows
        q = q_ref[c * rows:(c + 1) * rows, :]
        zero = jnp.zeros_like(q)
        qs = jnp.concatenate([jnp.where(lane < HEAD_DIM, q, zero),
                              jnp.where(lane >= HEAD_DIM, q, zero)], axis=0)
        s = lax.dot_general(qs, k_ref[0:keys, :], (((1,), (1,)), ((), ())),
                            preferred_element_type=F32)
        s = add_bias(s)
        m = jnp.max(s, axis=1, keepdims=True)
        p = jnp.exp(s - m)
        l = jnp.sum(p, axis=1, keepdims=True)
        acc = jnp.dot(p.astype(BF16), v_ref[0:keys, :], preferred_element_type=F32)
        emit(c, acc, l)


def _dilated_attn_kernel(q_ref, k_ref, v_ref, bias_ref, o_ref):
    rows = ATT_ROWS
    seq = q_ref.shape[0]
    lane = lax.broadcasted_iota(jnp.int32, (rows, LANES), 1)

    def add_bias(s):
        b = bias_ref[:, seq - s.shape[1]:seq]
        return s + jnp.concatenate([b, b], axis=0)

    def emit(c, acc, l):
        o = acc * (1.0 / l)
        o = jnp.where(lane < HEAD_DIM, o[:rows], o[rows:])
        o_ref[c * rows:(c + 1) * rows, :] = o.astype(o_ref.dtype)

    _pair_attention(q_ref, k_ref, v_ref, add_bias, emit)


def _diff_attn_kernel(q_ref, k_ref, v_ref, bias_ref, lq1_ref, lk1_ref, lq2_ref, lk2_ref, g_ref,
                      o_ref, *, lam_init):
    rows = ATT_ROWS
    lam = (jnp.exp(jnp.sum(lq1_ref[...] * lk1_ref[...], axis=1, keepdims=True))
           - jnp.exp(jnp.sum(lq2_ref[...] * lk2_ref[...], axis=1, keepdims=True)) + lam_init)
    gain = g_ref[...]

    def add_bias(s):
        b = bias_ref[...]
        keys = s.shape[1]
        diag = s[:, keys - rows:] + jnp.concatenate([b, b], axis=0)
        return diag if keys == rows else jnp.concatenate([s[:, :keys - rows], diag], axis=1)

    def emit(c, acc, l):
        o = acc * (1.0 / l)
        o = o[:rows] - lam * o[rows:]
        o = _rms_norm(o, gain, SUBLN_EPS) * (1.0 - lam_init)
        o_ref[c * rows:(c + 1) * rows, :] = o.astype(o_ref.dtype)

    _pair_attention(q_ref, k_ref, v_ref, add_bias, emit)


def _attention_call(body, qkv, bias, extra, seq, n_pairs, q_off, k_off, v_off, name):
    t = qkv.shape[0]
    batch = t // seq
    small = lambda a: pl.BlockSpec(a.shape, lambda b, p: (0,) * a.ndim)
    col = lambda off: pl.BlockSpec((seq, LANES), lambda b, p: (b, off + p))
    return pl.pallas_call(
        body,
        grid=(batch, n_pairs),
        in_specs=[col(q_off), col(k_off), col(v_off), small(bias)] + [small(a) for a in extra],
        out_specs=col(0),
        out_shape=jax.ShapeDtypeStruct((t, n_pairs * LANES), BF16),
        compiler_params=pltpu.CompilerParams(dimension_semantics=("arbitrary", "arbitrary"),
                                             vmem_limit_bytes=VMEM_LIMIT_BYTES),
        name=name,
    )(qkv, qkv, qkv, bias, *extra)


OUT_ROWS = 512


def _out_proj_kernel(h_ref, oa_ref, ob_ref, oc_ref, w_ref, o_ref):
    mix = jnp.dot(oa_ref[...], w_ref[0:A_WIDTH, :], preferred_element_type=F32)
    mix += jnp.dot(ob_ref[...], w_ref[A_WIDTH:A_WIDTH + B_V, :], preferred_element_type=F32)
    mix += jnp.dot(oc_ref[...], w_ref[A_WIDTH + B_V:, :], preferred_element_type=F32)
    o_ref[...] = h_ref[...] + mix


def _out_proj(h, o_a, o_b, o_c, w):
    t = h.shape[0]
    rows = OUT_ROWS
    row_spec = lambda width: pl.BlockSpec((rows, width), lambda i: (i, 0))
    return pl.pallas_call(
        _out_proj_kernel,
        grid=(t // rows,),
        in_specs=[row_spec(D_MODEL), row_spec(A_WIDTH), row_spec(B_V), row_spec(C_WIDTH),
                  pl.BlockSpec(w.shape, lambda i: (0, 0))],
        out_specs=row_spec(D_MODEL),
        out_shape=jax.ShapeDtypeStruct((t, D_MODEL), F32),
        compiler_params=pltpu.CompilerParams(dimension_semantics=("arbitrary",),
                                             vmem_limit_bytes=VMEM_LIMIT_BYTES),
        name="out_proj",
    )(h, o_a, o_b, o_c, w)


FFN_ROWS = 512
FFN_COLS = D_FF // 2


def _ffn_kernel(h_ref, g_ref, wg_ref, wu_ref, cw_ref, wd_ref, fg_ref, o_ref,
                hn_ref, acc_ref, carry_ref, *, tiles_per_seq, final_norm):
    i = pl.program_id(0)
    j = pl.program_id(1)

    @pl.when(j == 0)
    def _():
        hn_ref[...] = _rms_norm(h_ref[...], g_ref[...], NORM_EPS).astype(BF16)
        acc_ref[...] = jnp.zeros_like(acc_ref)

    @pl.when(i % tiles_per_seq == 0)
    def _():
        carry_ref[j] = jnp.zeros(carry_ref.shape[1:], F32)

    hn = hn_ref[...]
    gate = jnp.dot(hn, wg_ref[...], preferred_element_type=F32)
    up = jnp.dot(hn, wu_ref[...], preferred_element_type=F32)
    prev = carry_ref[j]
    carry_ref[j] = gate[gate.shape[0] - CONV_HALO_ROWS:, :]
    gc = _causal_conv3(gate, prev, cw_ref[...])
    act = gc / (1.0 + jnp.exp(-gc)) * up
    acc_ref[...] += jnp.dot(act.astype(BF16), wd_ref[...], preferred_element_type=F32)

    @pl.when(j == pl.num_programs(1) - 1)
    def _():
        out = h_ref[...] + acc_ref[...]
        if final_norm:
            out = _rms_norm(out, fg_ref[...], NORM_EPS)
        o_ref[...] = out


def _ffn(h, g, w_up, conv_w, w_down, final_g, seq, final_norm):
    t = h.shape[0]
    rows, cols = FFN_ROWS, FFN_COLS
    nj = D_FF // cols
    return pl.pallas_call(
        functools.partial(_ffn_kernel, tiles_per_seq=seq // rows, final_norm=final_norm),
        grid=(t // rows, nj),
        in_specs=[pl.BlockSpec((rows, D_MODEL), lambda i, j: (i, 0)),
                  pl.BlockSpec((1, D_MODEL), lambda i, j: (0, 0)),
                  pl.BlockSpec((D_MODEL, cols), lambda i, j: (0, j)),
                  pl.BlockSpec((D_MODEL, cols), lambda i, j: (0, nj + j)),
                  pl.BlockSpec((CONV_WIDTH, cols), lambda i, j: (0, j)),
                  pl.BlockSpec((cols, D_MODEL), lambda i, j: (j, 0)),
                  pl.BlockSpec((1, D_MODEL), lambda i, j: (0, 0))],
        out_specs=pl.BlockSpec((rows, D_MODEL), lambda i, j: (i, 0)),
        out_shape=jax.ShapeDtypeStruct((t, D_MODEL), F32),
        scratch_shapes=[pltpu.VMEM((rows, D_MODEL), BF16),
                        pltpu.VMEM((rows, D_MODEL), F32),
                        pltpu.VMEM((nj, CONV_HALO_ROWS, cols), F32)],
        compiler_params=pltpu.CompilerParams(dimension_semantics=("arbitrary", "arbitrary"),
                                             vmem_limit_bytes=VMEM_LIMIT_BYTES),
        name="conv_glu_ffn",
    )(h, g, w_up, w_up, conv_w, w_down, final_g)


def kernel(x, positions, norm_mix_g, w_in, lambda_q1, lambda_k1, lambda_q2, lambda_k2, subln_g,
           conv_mix_w, w_out, norm_ffn_g, w_up, conv_ffn_w, w_down, final_g):
    batch, seq, d = x.shape
    depth = w_in.shape[0]
    assert d == D_MODEL and seq % ATT_ROWS == 0 and seq % IN_ROWS == 0 and seq % FFN_ROWS == 0
    t = batch * seq
    h = x.reshape(t, d)
    tables = _rope_tables(positions.reshape(t, 1))
    bias_a = jnp.asarray(_dilated_bias_table(seq))
    bias_b = jnp.asarray(_causal_bias_table())
    row = lambda v: v.reshape(1, -1)

    for layer in range(depth):
        a_qkv, b_qkv, o_c = _in_proj(h, row(norm_mix_g[layer]), w_in[layer].astype(BF16), tables,
                                     conv_mix_w[layer], seq)
        o_a = _attention_call(_dilated_attn_kernel, a_qkv, bias_a, [], seq,
                              n_pairs=A_HEADS // 2, q_off=0, k_off=2, v_off=4, name="dilated_attn")
        lam_init = 0.8 - 0.6 * math.exp(-0.3 * layer)
        o_b = _attention_call(
            functools.partial(_diff_attn_kernel, lam_init=lam_init), b_qkv, bias_b,
            [row(lambda_q1[layer]), row(lambda_k1[layer]), row(lambda_q2[layer]),
             row(lambda_k2[layer]), row(subln_g[layer])],
            seq, n_pairs=B_HEADS, q_off=0, k_off=B_HEADS, v_off=2 * B_HEADS, name="diff_attn")
        h = _out_proj(h, o_a, o_b, o_c, w_out[layer].astype(BF16))
        h = _ffn(h, row(norm_ffn_g[layer]), w_up[layer].astype(BF16), conv_ffn_w[layer],
                 w_down[layer].astype(BF16), row(final_g), seq, final_norm=(layer == depth - 1))
    return h.reshape(batch, seq, d)
```

```python
import functools
import math

import numpy as np
import jax
import jax.numpy as jnp
from jax import lax
from jax.experimental import pallas as pl
from jax.experimental.pallas import tpu as pltpu

D_MODEL = 1024
HEAD_DIM = 64
A_HEADS = 4
A_WIDTH = 256
B_HEADS = 4
B_QK = 512
B_V = 512
C_WIDTH = 256
IN_COLS = 3072
DILATED_PATTERNS = ((128, 1), (512, 4), (2048, 16))
CONV_WIDTH = 3
D_FF = 2816
ROPE_THETA = 500000.0
ROPE_DIMS = HEAD_DIM // 4
NORM_EPS = 1e-6
SUBLN_EPS = 1e-5

LANES = 128
CONV_HALO_ROWS = 8
MASK_BIAS = -1e30
LOG2_E = math.log2(math.e)
VMEM_LIMIT_BYTES = 56 * 1024 * 1024

F32 = jnp.float32
BF16 = jnp.bfloat16


def _rms_norm(x, g, eps):
    y = x * lax.rsqrt(jnp.mean(x * x, axis=-1, keepdims=True) + eps)
    return y * g


def _causal_conv3(p, prev, w):
    rows = lax.broadcasted_iota(jnp.int32, p.shape, 0)
    last = prev[CONV_HALO_ROWS - 1:CONV_HALO_ROWS, :]
    last2 = prev[CONV_HALO_ROWS - 2:CONV_HALO_ROWS - 1, :]
    p1 = jnp.where(rows == 0, last, pltpu.roll(p, 1, axis=0))
    p2 = jnp.where(rows == 0, last2, jnp.where(rows == 1, last, pltpu.roll(p, 2, axis=0)))
    return w[0:1, :] * p2 + w[1:2, :] * p1 + w[2:3, :] * p


def _rope_table_kernel(pos_ref, inv_ref, c_ref, s1_ref, s2_ref):
    ang = pos_ref[...].astype(F32) * inv_ref[...]
    lane = lax.broadcasted_iota(jnp.int32, ang.shape, 1) % HEAD_DIM
    c = jnp.cos(ang)
    s = jnp.sin(ang)
    half = ROPE_DIMS // 2
    c_ref[...] = jnp.where(lane < ROPE_DIMS, c, 1.0)
    s1_ref[...] = jnp.where((lane >= half) & (lane < ROPE_DIMS), s, 0.0)
    s2_ref[...] = jnp.where(lane < half, -s, 0.0)


def _rope_tables(pos_col):
    t = pos_col.shape[0]
    half = ROPE_DIMS // 2
    inv = ROPE_THETA ** (-(jnp.arange(half, dtype=F32) * 2.0 / ROPE_DIMS))
    lane = np.arange(LANES) % HEAD_DIM
    inv_lane = jnp.where(lane < ROPE_DIMS, inv[lane % half], 0.0).astype(F32)[None, :]
    rows = 1024
    out = jax.ShapeDtypeStruct((t, LANES), F32)
    return pl.pallas_call(
        _rope_table_kernel,
        grid=(t // rows,),
        in_specs=[pl.BlockSpec((rows, 1), lambda i: (i, 0)),
                  pl.BlockSpec((1, LANES), lambda i: (0, 0))],
        out_specs=[pl.BlockSpec((rows, LANES), lambda i: (i, 0))] * 3,
        out_shape=[out] * 3,
        name="rope_tables",
    )(pos_col, inv_lane)


IN_ROWS = 512


def _in_proj_kernel(h_ref, g_ref, w_ref, c_ref, s1_ref, s2_ref, cw_ref,
                    a_ref, b_ref, oc_ref, hn_ref, carry_ref, *, tiles_per_seq):
    i = pl.program_id(0)
    hn_ref[...] = _rms_norm(h_ref[...], g_ref[...], NORM_EPS).astype(BF16)

    cos = c_ref[...]
    sin_up = s1_ref[...]
    sin_dn = s2_ref[...]
    half = ROPE_DIMS // 2

    def proj(c0, width):
        return jnp.dot(hn_ref[...], w_ref[:, c0:c0 + width], preferred_element_type=F32)

    def rope(y):
        return (y * cos + pltpu.roll(y, half, axis=1) * sin_up
                + pltpu.roll(y, LANES - half, axis=1) * sin_dn)

    def rope_cols(c0, width, scale, out_ref, o0):
        y = proj(c0, width)
        for k in range(width // LANES):
            r = rope(y[:, k * LANES:(k + 1) * LANES])
            if scale != 1.0:
                r = r * scale
            out_ref[:, o0 + k * LANES:o0 + (k + 1) * LANES] = r.astype(BF16)

    scale = HEAD_DIM ** -0.5 * LOG2_E
    rope_cols(0, A_WIDTH, scale, a_ref, 0)
    rope_cols(A_WIDTH, A_WIDTH, 1.0, a_ref, A_WIDTH)
    a_ref[:, 2 * A_WIDTH:3 * A_WIDTH] = proj(2 * A_WIDTH, A_WIDTH).astype(BF16)
    b0 = 3 * A_WIDTH
    rope_cols(b0, B_QK, scale, b_ref, 0)
    rope_cols(b0 + B_QK, B_QK, 1.0, b_ref, B_QK)
    b_ref[:, 2 * B_QK:2 * B_QK + B_V] = proj(b0 + 2 * B_QK, B_V).astype(BF16)

    c0 = b0 + 2 * B_QK + B_V
    gate = proj(c0, C_WIDTH)
    prod = proj(c0 + C_WIDTH, C_WIDTH) * proj(c0 + 2 * C_WIDTH, C_WIDTH)

    @pl.when(i % tiles_per_seq == 0)
    def _():
        carry_ref[...] = jnp.zeros_like(carry_ref)

    prev = carry_ref[...]
    carry_ref[...] = prod[prod.shape[0] - CONV_HALO_ROWS:, :]
    oc_ref[...] = (gate * _causal_conv3(prod, prev, cw_ref[...])).astype(BF16)


def _layer_weight_spec(w, layer):
    return pl.BlockSpec((None,) + w.shape[1:], lambda i: (layer, 0, 0),
                        pipeline_mode=pl.Buffered(1))


def _in_proj(h, g, w, layer, tables, conv_w, seq):
    t = h.shape[0]
    rows = IN_ROWS
    cos, s1, s2 = tables
    row_spec = lambda width: pl.BlockSpec((rows, width), lambda i: (i, 0))
    full = lambda shape: pl.BlockSpec(shape, lambda i: (0, 0))
    return pl.pallas_call(
        functools.partial(_in_proj_kernel, tiles_per_seq=seq // rows),
        grid=(t // rows,),
        in_specs=[row_spec(D_MODEL), full((1, D_MODEL)), _layer_weight_spec(w, layer),
                  row_spec(LANES), row_spec(LANES), row_spec(LANES), full((CONV_WIDTH, C_WIDTH))],
        out_specs=[row_spec(3 * A_WIDTH), row_spec(2 * B_QK + B_V), row_spec(C_WIDTH)],
        out_shape=[jax.ShapeDtypeStruct((t, 3 * A_WIDTH), BF16),
                   jax.ShapeDtypeStruct((t, 2 * B_QK + B_V), BF16),
                   jax.ShapeDtypeStruct((t, C_WIDTH), BF16)],
        scratch_shapes=[pltpu.VMEM((rows, D_MODEL), BF16),
                        pltpu.VMEM((CONV_HALO_ROWS, C_WIDTH), F32)],
        compiler_params=pltpu.CompilerParams(dimension_semantics=("arbitrary",),
                                             vmem_limit_bytes=VMEM_LIMIT_BYTES),
        name="in_proj",
    )(h, g, w, cos, s1, s2, conv_w)


ATT_ROWS = 256
SCORE_SLOTS = 2
ONES_ROWS = 16


def _dilated_bias_table(seq):
    u = np.arange(seq)[:, None]
    r = np.arange(ATT_ROWS)[None, :]
    delta = r + seq - ATT_ROWS - u
    count = np.zeros(delta.shape, np.int64)
    for window, dil in DILATED_PATTERNS:
        count += (delta >= 0) & (delta <= window) & (delta % dil == 0)
    return np.where(count > 0, np.log2(np.maximum(count, 1)), MASK_BIAS).astype(np.float32)


def _causal_bias_table():
    key = np.arange(ATT_ROWS)[:, None]
    query = np.arange(ATT_ROWS)[None, :]
    return np.where(key <= query, 0.0, MASK_BIAS).astype(np.float32)


def _pair_attention(q_ref, k_ref, v_ref, vt_ref, s_ref, add_bias, emit):
    rows = ATT_ROWS
    n_chunks = q_ref.shape[0] // rows
    vt_ref[0:LANES, :] = v_ref[...].astype(F32).T.astype(BF16)
    vt_ref[LANES:, :] = jnp.ones((ONES_ROWS, vt_ref.shape[1]), BF16)
    lane = lax.broadcasted_iota(jnp.int32, (rows, LANES), 1)

    def stacked_queries(c):
        q = q_ref[c * rows:(c + 1) * rows, :]
        zero = jnp.zeros_like(q)
        return jnp.concatenate([jnp.where(lane < HEAD_DIM, q, zero),
                                jnp.where(lane >= HEAD_DIM, q, zero)], axis=0)

    def score_block(c, j, qs, col_max):
        s = lax.dot_general(k_ref[j * rows:(j + 1) * rows, :], qs, (((1,), (1,)), ((), ())),
                            preferred_element_type=F32)
        s = add_bias(s, c, j)
        s_ref[c % SCORE_SLOTS, j * rows:(j + 1) * rows, :] = s
        blk_max = jnp.max(s.reshape(rows // 8, 8, 2 * rows), axis=0)
        return blk_max if col_max is None else jnp.maximum(col_max, blk_max)

    def value_block(c, j, m, acc):
        p = jnp.exp2(s_ref[c % SCORE_SLOTS, j * rows:(j + 1) * rows, :] - m).astype(BF16)
        part = jnp.dot(vt_ref[:, j * rows:(j + 1) * rows], p, preferred_element_type=F32)
        return part if acc is None else acc + part

    qs = stacked_queries(0)
    col_max = score_block(0, 0, qs, None)
    acc = None
    for c in range(n_chunks):
        m = jnp.max(col_max, axis=0, keepdims=True)
        nxt = c + 1
        if nxt < n_chunks:
            qs = stacked_queries(nxt)
        acc = None
        col_max = None
        for j in range(nxt + 1):
            if nxt < n_chunks:
                col_max = score_block(nxt, j, qs, col_max)
            if j <= c:
                acc = value_block(c, j, m, acc)
        emit(c, acc[0:LANES], acc[LANES:LANES + 1])


def _dilated_attn_kernel(q_ref, k_ref, v_ref, bias_ref, o_ref, vt_ref, s_ref):
    rows = ATT_ROWS
    seq = q_ref.shape[0]

    def add_bias(s, c, j):
        first = seq - (c + 1 - j) * rows
        b = bias_ref[first:first + rows, :]
        return s + jnp.concatenate([b, b], axis=1)

    def emit(c, o_t, l):
        o_t = o_t * (1.0 / l)
        o = jnp.concatenate([o_t[:HEAD_DIM, :rows], o_t[HEAD_DIM:, rows:]], axis=0)
        o_ref[c * rows:(c + 1) * rows, :] = o.T.astype(o_ref.dtype)

    _pair_attention(q_ref, k_ref, v_ref, vt_ref, s_ref, add_bias, emit)


def _diff_attn_kernel(q_ref, k_ref, v_ref, bias_ref, lq1_ref, lk1_ref, lq2_ref, lk2_ref, g_ref,
                      o_ref, vt_ref, s_ref, *, lam_init):
    rows = ATT_ROWS
    lam = (jnp.exp(jnp.sum(lq1_ref[...] * lk1_ref[...], axis=1, keepdims=True))
           - jnp.exp(jnp.sum(lq2_ref[...] * lk2_ref[...], axis=1, keepdims=True)) + lam_init)
    gain = g_ref[...]

    def add_bias(s, c, j):
        if j < c:
            return s
        b = bias_ref[...]
        return s + jnp.concatenate([b, b], axis=1)

    def emit(c, o_t, l):
        o_t = o_t * (1.0 / l)
        o = o_t[:, :rows] - lam * o_t[:, rows:]
        o = o * lax.rsqrt(jnp.mean(o * o, axis=0, keepdims=True) + SUBLN_EPS) * gain
        o_ref[c * rows:(c + 1) * rows, :] = (o * (1.0 - lam_init)).T.astype(o_ref.dtype)

    _pair_attention(q_ref, k_ref, v_ref, vt_ref, s_ref, add_bias, emit)


def _attention_call(body, qkv, bias, extra, seq, n_pairs, q_off, k_off, v_off, name):
    t = qkv.shape[0]
    batch = t // seq
    small = lambda a: pl.BlockSpec(a.shape, lambda b, p: (0,) * a.ndim)
    col = lambda off: pl.BlockSpec((seq, LANES), lambda b, p: (b, off + p))
    return pl.pallas_call(
        body,
        grid=(batch, n_pairs),
        in_specs=[col(q_off), col(k_off), col(v_off), small(bias)] + [small(a) for a in extra],
        out_specs=col(0),
        out_shape=jax.ShapeDtypeStruct((t, n_pairs * LANES), BF16),
        scratch_shapes=[pltpu.VMEM((LANES + ONES_ROWS, seq), BF16),
                        pltpu.VMEM((SCORE_SLOTS, seq, 2 * ATT_ROWS), F32)],
        compiler_params=pltpu.CompilerParams(dimension_semantics=("arbitrary", "arbitrary"),
                                             vmem_limit_bytes=VMEM_LIMIT_BYTES),
        name=name,
    )(qkv, qkv, qkv, bias, *extra)


FFN_ROWS = 512
FFN_CHUNK = 256


def _mix_ffn_kernel(h_ref, oa_ref, ob_ref, oc_ref, wo_ref, g_ref, wup_ref, cw_ref, wd_ref, fg_ref,
                    o_ref, hn_ref, act_ref, carry_ref, *, tiles_per_seq, final_norm):
    i = pl.program_id(0)
    mix = jnp.dot(oa_ref[...], wo_ref[0:A_WIDTH, :], preferred_element_type=F32)
    mix += jnp.dot(ob_ref[...], wo_ref[A_WIDTH:A_WIDTH + B_V, :], preferred_element_type=F32)
    mix += jnp.dot(oc_ref[...], wo_ref[A_WIDTH + B_V:, :], preferred_element_type=F32)
    h_mid = h_ref[...] + mix
    o_ref[...] = h_mid
    hn_ref[...] = _rms_norm(h_mid, g_ref[...], NORM_EPS).astype(BF16)

    @pl.when(i % tiles_per_seq == 0)
    def _():
        carry_ref[...] = jnp.zeros_like(carry_ref)

    def gate_up(c):
        cols = slice(c * FFN_CHUNK, (c + 1) * FFN_CHUNK)
        up_cols = slice(D_FF + c * FFN_CHUNK, D_FF + (c + 1) * FFN_CHUNK)
        hn = hn_ref[...]
        return (jnp.dot(hn, wup_ref[:, cols], preferred_element_type=F32),
                jnp.dot(hn, wup_ref[:, up_cols], preferred_element_type=F32))

    def activate(c, gate, up):
        cols = slice(c * FFN_CHUNK, (c + 1) * FFN_CHUNK)
        prev = carry_ref[:, cols]
        carry_ref[:, cols] = gate[gate.shape[0] - CONV_HALO_ROWS:, :]
        gc = _causal_conv3(gate, prev, cw_ref[:, cols])
        act_ref[:, cols] = (gc / (1.0 + jnp.exp(-gc)) * up).astype(BF16)

    n_chunks = D_FF // FFN_CHUNK
    pending = gate_up(0)
    for c in range(n_chunks):
        nxt = gate_up(c + 1) if c + 1 < n_chunks else None
        activate(c, *pending)
        pending = nxt

    out = o_ref[...] + jnp.dot(act_ref[...], wd_ref[...], preferred_element_type=F32)
    if final_norm:
        out = _rms_norm(out, fg_ref[...], NORM_EPS)
    o_ref[...] = out


def _mix_ffn(h, o_a, o_b, o_c, w_out, g, w_up, conv_w, w_down, final_g, layer, seq, final_norm):
    t = h.shape[0]
    rows = FFN_ROWS
    row_spec = lambda width: pl.BlockSpec((rows, width), lambda i: (i, 0))
    full = lambda shape: pl.BlockSpec(shape, lambda i: (0, 0))
    return pl.pallas_call(
        functools.partial(_mix_ffn_kernel, tiles_per_seq=seq // rows, final_norm=final_norm),
        grid=(t // rows,),
        in_specs=[row_spec(D_MODEL), row_spec(A_WIDTH), row_spec(B_V), row_spec(C_WIDTH),
                  _layer_weight_spec(w_out, layer), full((1, D_MODEL)),
                  _layer_weight_spec(w_up, layer), full((CONV_WIDTH, D_FF)),
                  _layer_weight_spec(w_down, layer), full((1, D_MODEL))],
        out_specs=row_spec(D_MODEL),
        out_shape=jax.ShapeDtypeStruct((t, D_MODEL), F32),
        scratch_shapes=[pltpu.VMEM((rows, D_MODEL), BF16),
                        pltpu.VMEM((rows, D_FF), BF16),
                        pltpu.VMEM((CONV_HALO_ROWS, D_FF), F32)],
        compiler_params=pltpu.CompilerParams(dimension_semantics=("arbitrary",),
                                             vmem_limit_bytes=VMEM_LIMIT_BYTES),
        name="mix_ffn",
    )(h, o_a, o_b, o_c, w_out, g, w_up, conv_w, w_down, final_g)


def kernel(x, positions, norm_mix_g, w_in, lambda_q1, lambda_k1, lambda_q2, lambda_k2, subln_g,
           conv_mix_w, w_out, norm_ffn_g, w_up, conv_ffn_w, w_down, final_g):
    batch, seq, d = x.shape
    depth = w_in.shape[0]
    assert d == D_MODEL and seq % ATT_ROWS == 0 and seq % IN_ROWS == 0 and seq % FFN_ROWS == 0
    t = batch * seq
    h = x.reshape(t, d)
    tables = _rope_tables(positions.reshape(t, 1))
    bias_a = jnp.asarray(_dilated_bias_table(seq))
    bias_b = jnp.asarray(_causal_bias_table())
    row = lambda v: v.reshape(1, -1)
    w_in, w_out, w_up, w_down = (w.astype(BF16) for w in (w_in, w_out, w_up, w_down))

    for layer in range(depth):
        a_qkv, b_qkv, o_c = _in_proj(h, row(norm_mix_g[layer]), w_in, layer, tables,
                                     conv_mix_w[layer], seq)
        o_a = _attention_call(_dilated_attn_kernel, a_qkv, bias_a, [], seq,
                              n_pairs=A_HEADS // 2, q_off=0, k_off=2, v_off=4, name="dilated_attn")
        lam_init = 0.8 - 0.6 * math.exp(-0.3 * layer)
        o_b = _attention_call(
            functools.partial(_diff_attn_kernel, lam_init=lam_init), b_qkv, bias_b,
            [row(lambda_q1[layer]), row(lambda_k1[layer]), row(lambda_q2[layer]),
             row(lambda_k2[layer]), subln_g[layer].reshape(-1, 1)],
            seq, n_pairs=B_HEADS, q_off=0, k_off=B_HEADS, v_off=2 * B_HEADS, name="diff_attn")
        h = _mix_ffn(h, o_a, o_b, o_c, w_out, row(norm_ffn_g[layer]), w_up, conv_ffn_w[layer], w_down,
                     row(final_g), layer, seq, final_norm=(layer == depth - 1))
    return h.reshape(batch, seq, d)
```

```python
import functools
import math

import numpy as np
import jax
import jax.numpy as jnp
from jax import lax
from jax.experimental import pallas as pl
from jax.experimental.pallas import tpu as pltpu

D_MODEL = 1024
HEAD_DIM = 64
A_HEADS = 4
A_WIDTH = 256
B_HEADS = 4
B_QK = 512
B_V = 512
C_WIDTH = 256
IN_COLS = 3072
DILATED_PATTERNS = ((128, 1), (512, 4), (2048, 16))
CONV_WIDTH = 3
D_FF = 2816
ROPE_THETA = 500000.0
ROPE_DIMS = HEAD_DIM // 4
NORM_EPS = 1e-6
SUBLN_EPS = 1e-5

LANES = 128
CONV_HALO_ROWS = 8
MASK_BIAS = -1e30
LOG2_E = math.log2(math.e)
VMEM_LIMIT_BYTES = 56 * 1024 * 1024

F32 = jnp.float32
BF16 = jnp.bfloat16


def _rms_norm(x, g, eps):
    y = x * lax.rsqrt(jnp.mean(x * x, axis=-1, keepdims=True) + eps)
    return y * g


def _causal_conv3(p, prev, w):
    rows = lax.broadcasted_iota(jnp.int32, p.shape, 0)
    last = prev[CONV_HALO_ROWS - 1:CONV_HALO_ROWS, :]
    last2 = prev[CONV_HALO_ROWS - 2:CONV_HALO_ROWS - 1, :]
    p1 = jnp.where(rows == 0, last, pltpu.roll(p, 1, axis=0))
    p2 = jnp.where(rows == 0, last2, jnp.where(rows == 1, last, pltpu.roll(p, 2, axis=0)))
    return w[0:1, :] * p2 + w[1:2, :] * p1 + w[2:3, :] * p


ROPE_TOKENS_PER_ROW = LANES // ROPE_DIMS


def _rope_table_kernel(pos_ref, inv_ref, o_ref):
    ang = pos_ref[...].astype(F32) * inv_ref[...]
    lane = lax.broadcasted_iota(jnp.int32, ang.shape, 1) % ROPE_DIMS
    o_ref[...] = jnp.where(lane < ROPE_DIMS // 2, jnp.cos(ang), jnp.sin(ang))


def _rope_tables(pos):
    t = pos.shape[0]
    half = ROPE_DIMS // 2
    inv = ROPE_THETA ** (-(jnp.arange(half, dtype=F32) * 2.0 / ROPE_DIMS))
    inv_lane = jnp.tile(inv, LANES // half)[None, :]
    rows = t // ROPE_TOKENS_PER_ROW
    pos_rep = jnp.repeat(pos.reshape(rows, ROPE_TOKENS_PER_ROW), ROPE_DIMS, axis=1)
    compact = pl.pallas_call(
        _rope_table_kernel,
        grid=(1,),
        in_specs=[pl.BlockSpec((rows, LANES), lambda i: (0, 0)),
                  pl.BlockSpec((1, LANES), lambda i: (0, 0))],
        out_specs=pl.BlockSpec((rows, LANES), lambda i: (0, 0)),
        out_shape=jax.ShapeDtypeStruct((rows, LANES), F32),
        name="rope_tables",
    )(pos_rep, inv_lane)
    cs = compact.reshape(t, ROPE_DIMS)
    c, s = cs[:, :half], cs[:, half:]
    fill = lambda value, width: jnp.full((t, width), value, F32)
    per_head = lambda parts: jnp.tile(jnp.concatenate(parts, axis=1), (1, LANES // HEAD_DIM))
    return (per_head([c, c, fill(1.0, HEAD_DIM - ROPE_DIMS)]),
            per_head([fill(0.0, half), s, fill(0.0, HEAD_DIM - ROPE_DIMS)]),
            per_head([-s, fill(0.0, HEAD_DIM - half)]))


IN_ROWS = 512


def _in_proj_kernel(h_ref, g_ref, w_ref, c_ref, s1_ref, s2_ref, cw_ref,
                    a_ref, b_ref, oc_ref, hn_ref, carry_ref, *, tiles_per_seq):
    i = pl.program_id(0)
    hn_ref[...] = _rms_norm(h_ref[...], g_ref[...], NORM_EPS).astype(BF16)

    cos = c_ref[...]
    sin_up = s1_ref[...]
    sin_dn = s2_ref[...]
    half = ROPE_DIMS // 2

    def proj(c0, width):
        return jnp.dot(hn_ref[...], w_ref[:, c0:c0 + width], preferred_element_type=F32)

    def rope(y):
        return (y * cos + pltpu.roll(y, half, axis=1) * sin_up
                + pltpu.roll(y, LANES - half, axis=1) * sin_dn)

    def rope_cols(c0, width, scale, out_ref, o0):
        y = proj(c0, width)
        for k in range(width // LANES):
            r = rope(y[:, k * LANES:(k + 1) * LANES])
            if scale != 1.0:
                r = r * scale
            out_ref[:, o0 + k * LANES:o0 + (k + 1) * LANES] = r.astype(BF16)

    scale = HEAD_DIM ** -0.5 * LOG2_E
    rope_cols(0, A_WIDTH, scale, a_ref, 0)
    rope_cols(A_WIDTH, A_WIDTH, 1.0, a_ref, A_WIDTH)
    a_ref[:, 2 * A_WIDTH:3 * A_WIDTH] = proj(2 * A_WIDTH, A_WIDTH).astype(BF16)
    b0 = 3 * A_WIDTH
    rope_cols(b0, B_QK, scale, b_ref, 0)
    rope_cols(b0 + B_QK, B_QK, 1.0, b_ref, B_QK)
    b_ref[:, 2 * B_QK:2 * B_QK + B_V] = proj(b0 + 2 * B_QK, B_V).astype(BF16)

    c0 = b0 + 2 * B_QK + B_V
    gate = proj(c0, C_WIDTH)
    prod = proj(c0 + C_WIDTH, C_WIDTH) * proj(c0 + 2 * C_WIDTH, C_WIDTH)

    @pl.when(i % tiles_per_seq == 0)
    def _():
        carry_ref[...] = jnp.zeros_like(carry_ref)

    prev = carry_ref[...]
    carry_ref[...] = prod[prod.shape[0] - CONV_HALO_ROWS:, :]
    oc_ref[...] = (gate * _causal_conv3(prod, prev, cw_ref[...])).astype(BF16)


def _layer_weight_spec(w, layer):
    return pl.BlockSpec((None,) + w.shape[1:], lambda i: (layer, 0, 0),
                        pipeline_mode=pl.Buffered(1))


def _in_proj(h, g, w, layer, tables, conv_w, seq):
    t = h.shape[0]
    rows = IN_ROWS
    cos, s1, s2 = tables
    row_spec = lambda width: pl.BlockSpec((rows, width), lambda i: (i, 0))
    full = lambda shape: pl.BlockSpec(shape, lambda i: (0, 0))
    return pl.pallas_call(
        functools.partial(_in_proj_kernel, tiles_per_seq=seq // rows),
        grid=(t // rows,),
        in_specs=[row_spec(D_MODEL), full((1, D_MODEL)), _layer_weight_spec(w, layer),
                  row_spec(LANES), row_spec(LANES), row_spec(LANES), full((CONV_WIDTH, C_WIDTH))],
        out_specs=[row_spec(3 * A_WIDTH), row_spec(2 * B_QK + B_V), row_spec(C_WIDTH)],
        out_shape=[jax.ShapeDtypeStruct((t, 3 * A_WIDTH), BF16),
                   jax.ShapeDtypeStruct((t, 2 * B_QK + B_V), BF16),
                   jax.ShapeDtypeStruct((t, C_WIDTH), BF16)],
        scratch_shapes=[pltpu.VMEM((rows, D_MODEL), BF16),
                        pltpu.VMEM((CONV_HALO_ROWS, C_WIDTH), F32)],
        compiler_params=pltpu.CompilerParams(dimension_semantics=("arbitrary",),
                                             vmem_limit_bytes=VMEM_LIMIT_BYTES),
        name="in_proj",
    )(h, g, w, cos, s1, s2, conv_w)


ATT_ROWS = 256
SCORE_SLOTS = 2
ONES_ROWS = 16


def _dilated_bias_table(seq):
    u = np.arange(seq)[:, None]
    r = np.arange(ATT_ROWS)[None, :]
    delta = r + seq - ATT_ROWS - u
    count = np.zeros(delta.shape, np.int64)
    for window, dil in DILATED_PATTERNS:
        count += (delta >= 0) & (delta <= window) & (delta % dil == 0)
    return np.where(count > 0, np.log2(np.maximum(count, 1)), MASK_BIAS).astype(np.float32)


def _causal_bias_table():
    key = np.arange(ATT_ROWS)[:, None]
    query = np.arange(ATT_ROWS)[None, :]
    return np.where(key <= query, 0.0, MASK_BIAS).astype(np.float32)


def _pair_attention(q_ref, k_ref, v_ref, vt_ref, s_refs, add_bias, emit):
    rows = ATT_ROWS
    n_chunks = q_ref.shape[0] // rows
    vt_ref[0:LANES, :] = v_ref[...].astype(F32).T.astype(BF16)
    vt_ref[LANES:, :] = jnp.ones((ONES_ROWS, vt_ref.shape[1]), BF16)
    lane = lax.broadcasted_iota(jnp.int32, (rows, LANES), 1)

    def stacked_queries(c):
        q = q_ref[c * rows:(c + 1) * rows, :]
        zero = jnp.zeros_like(q)
        return jnp.concatenate([jnp.where(lane < HEAD_DIM, q, zero),
                                jnp.where(lane >= HEAD_DIM, q, zero)], axis=0)

    def score_block(c, j, qs, col_max):
        s = lax.dot_general(k_ref[j * rows:(j + 1) * rows, :], qs, (((1,), (1,)), ((), ())),
                            preferred_element_type=F32)
        s = add_bias(s, c, j)
        s_refs[c % SCORE_SLOTS][j * rows:(j + 1) * rows, :] = s
        blk_max = jnp.max(s.reshape(rows // 8, 8, 2 * rows), axis=0)
        return blk_max if col_max is None else jnp.maximum(col_max, blk_max)

    def value_block(c, j, m, acc):
        p = jnp.exp2(s_refs[c % SCORE_SLOTS][j * rows:(j + 1) * rows, :] - m).astype(BF16)
        part = jnp.dot(vt_ref[:, j * rows:(j + 1) * rows], p, preferred_element_type=F32)
        return part if acc is None else acc + part

    qs = stacked_queries(0)
    col_max = score_block(0, 0, qs, None)
    acc = None
    for c in range(n_chunks):
        m = jnp.max(col_max, axis=0, keepdims=True)
        nxt = c + 1
        if nxt < n_chunks:
            qs = stacked_queries(nxt)
        acc = None
        col_max = None
        for j in range(nxt + 1):
            if nxt < n_chunks:
                col_max = score_block(nxt, j, qs, col_max)
            if j <= c:
                acc = value_block(c, j, m, acc)
        emit(c, acc[0:LANES], acc[LANES:LANES + 1])


def _dilated_attn_kernel(q_ref, k_ref, v_ref, bias_ref, o_ref, vt_ref, *s_refs):
    rows = ATT_ROWS
    seq = q_ref.shape[0]

    def add_bias(s, c, j):
        first = seq - (c + 1 - j) * rows
        b = bias_ref[first:first + rows, :]
        return s + jnp.concatenate([b, b], axis=1)

    def emit(c, o_t, l):
        o_t = o_t * (1.0 / l)
        o = jnp.concatenate([o_t[:HEAD_DIM, :rows], o_t[HEAD_DIM:, rows:]], axis=0)
        o_ref[c * rows:(c + 1) * rows, :] = o.T.astype(o_ref.dtype)

    _pair_attention(q_ref, k_ref, v_ref, vt_ref, s_refs, add_bias, emit)


def _diff_attn_kernel(q_ref, k_ref, v_ref, bias_ref, lq1_ref, lk1_ref, lq2_ref, lk2_ref, g_ref,
                      o_ref, vt_ref, *s_refs, lam_init):
    rows = ATT_ROWS
    lam = (jnp.exp(jnp.sum(lq1_ref[...] * lk1_ref[...], axis=1, keepdims=True))
           - jnp.exp(jnp.sum(lq2_ref[...] * lk2_ref[...], axis=1, keepdims=True)) + lam_init)
    gain = g_ref[...]

    def add_bias(s, c, j):
        if j < c:
            return s
        b = bias_ref[...]
        return s + jnp.concatenate([b, b], axis=1)

    def emit(c, o_t, l):
        o_t = o_t * (1.0 / l)
        o = o_t[:, :rows] - lam * o_t[:, rows:]
        o = o * lax.rsqrt(jnp.mean(o * o, axis=0, keepdims=True) + SUBLN_EPS) * gain
        o_ref[c * rows:(c + 1) * rows, :] = (o * (1.0 - lam_init)).T.astype(o_ref.dtype)

    _pair_attention(q_ref, k_ref, v_ref, vt_ref, s_refs, add_bias, emit)


def _attention_call(body, qkv, bias, extra, seq, n_pairs, q_off, k_off, v_off, name):
    t = qkv.shape[0]
    batch = t // seq
    small = lambda a: pl.BlockSpec(a.shape, lambda b, p: (0,) * a.ndim)
    col = lambda off: pl.BlockSpec((seq, LANES), lambda b, p: (b, off + p))
    return pl.pallas_call(
        body,
        grid=(batch, n_pairs),
        in_specs=[col(q_off), col(k_off), col(v_off), small(bias)] + [small(a) for a in extra],
        out_specs=col(0),
        out_shape=jax.ShapeDtypeStruct((t, n_pairs * LANES), BF16),
        scratch_shapes=([pltpu.VMEM((LANES + ONES_ROWS, seq), BF16)]
                        + [pltpu.VMEM((seq, 2 * ATT_ROWS), F32)] * SCORE_SLOTS),
        compiler_params=pltpu.CompilerParams(dimension_semantics=("arbitrary", "arbitrary"),
                                             vmem_limit_bytes=VMEM_LIMIT_BYTES),
        name=name,
    )(qkv, qkv, qkv, bias, *extra)


FFN_ROWS = 512
FFN_CHUNK = 512


def _mix_ffn_kernel(h_ref, oa_ref, ob_ref, oc_ref, wo_ref, g_ref, wup_ref, cw_ref, wd_ref, fg_ref,
                    o_ref, hn_ref, act_ref, carry_ref, *, tiles_per_seq, final_norm):
    i = pl.program_id(0)
    mix = jnp.dot(oa_ref[...], wo_ref[0:A_WIDTH, :], preferred_element_type=F32)
    mix += jnp.dot(ob_ref[...], wo_ref[A_WIDTH:A_WIDTH + B_V, :], preferred_element_type=F32)
    mix += jnp.dot(oc_ref[...], wo_ref[A_WIDTH + B_V:, :], preferred_element_type=F32)
    h_mid = h_ref[...] + mix
    o_ref[...] = h_mid
    hn_ref[...] = _rms_norm(h_mid, g_ref[...], NORM_EPS).astype(BF16)

    @pl.when(i % tiles_per_seq == 0)
    def _():
        carry_ref[...] = jnp.zeros_like(carry_ref)

    def gate_up(cols):
        up_cols = slice(D_FF + cols.start, D_FF + cols.stop)
        hn = hn_ref[...]
        return (jnp.dot(hn, wup_ref[:, cols], preferred_element_type=F32),
                jnp.dot(hn, wup_ref[:, up_cols], preferred_element_type=F32))

    def activate(cols, gate, up):
        prev = carry_ref[:, cols]
        carry_ref[:, cols] = gate[gate.shape[0] - CONV_HALO_ROWS:, :]
        gc = _causal_conv3(gate, prev, cw_ref[:, cols])
        act_ref[:, cols] = (gc / (1.0 + jnp.exp(-gc)) * up).astype(BF16)

    chunks = [slice(c0, min(c0 + FFN_CHUNK, D_FF)) for c0 in range(0, D_FF, FFN_CHUNK)]
    pending = gate_up(chunks[0])
    for c, cols in enumerate(chunks):
        nxt = gate_up(chunks[c + 1]) if c + 1 < len(chunks) else None
        activate(cols, *pending)
        pending = nxt

    out = o_ref[...] + jnp.dot(act_ref[...], wd_ref[...], preferred_element_type=F32)
    if final_norm:
        out = _rms_norm(out, fg_ref[...], NORM_EPS)
    o_ref[...] = out


def _mix_ffn(h, o_a, o_b, o_c, w_out, g, w_up, conv_w, w_down, final_g, layer, seq, final_norm):
    t = h.shape[0]
    rows = FFN_ROWS
    row_spec = lambda width: pl.BlockSpec((rows, width), lambda i: (i, 0))
    full = lambda shape: pl.BlockSpec(shape, lambda i: (0, 0))
    return pl.pallas_call(
        functools.partial(_mix_ffn_kernel, tiles_per_seq=seq // rows, final_norm=final_norm),
        grid=(t // rows,),
        in_specs=[row_spec(D_MODEL), row_spec(A_WIDTH), row_spec(B_V), row_spec(C_WIDTH),
                  _layer_weight_spec(w_out, layer), full((1, D_MODEL)),
                  _layer_weight_spec(w_up, layer), full((CONV_WIDTH, D_FF)),
                  _layer_weight_spec(w_down, layer), full((1, D_MODEL))],
        out_specs=row_spec(D_MODEL),
        out_shape=jax.ShapeDtypeStruct((t, D_MODEL), F32),
        scratch_shapes=[pltpu.VMEM((rows, D_MODEL), BF16),
                        pltpu.VMEM((rows, D_FF), BF16),
                        pltpu.VMEM((CONV_HALO_ROWS, D_FF), F32)],
        compiler_params=pltpu.CompilerParams(dimension_semantics=("arbitrary",),
                                             vmem_limit_bytes=VMEM_LIMIT_BYTES),
        name="mix_ffn",
    )(h, o_a, o_b, o_c, w_out, g, w_up, conv_w, w_down, final_g)


def kernel(x, positions, norm_mix_g, w_in, lambda_q1, lambda_k1, lambda_q2, lambda_k2, subln_g,
           conv_mix_w, w_out, norm_ffn_g, w_up, conv_ffn_w, w_down, final_g):
    batch, seq, d = x.shape
    depth = w_in.shape[0]
    assert d == D_MODEL and seq % ATT_ROWS == 0 and seq % IN_ROWS == 0 and seq % FFN_ROWS == 0
    t = batch * seq
    h = x.reshape(t, d)
    tables = _rope_tables(positions.reshape(t))
    bias_a = jnp.asarray(_dilated_bias_table(seq))
    bias_b = jnp.asarray(_causal_bias_table())
    row = lambda v: v.reshape(1, -1)
    w_in, w_out, w_up, w_down = (w.astype(BF16) for w in (w_in, w_out, w_up, w_down))

    for layer in range(depth):
        a_qkv, b_qkv, o_c = _in_proj(h, row(norm_mix_g[layer]), w_in, layer, tables,
                                     conv_mix_w[layer], seq)
        o_a = _attention_call(_dilated_attn_kernel, a_qkv, bias_a, [], seq,
                              n_pairs=A_HEADS // 2, q_off=0, k_off=2, v_off=4, name="dilated_attn")
        lam_init = 0.8 - 0.6 * math.exp(-0.3 * layer)
        o_b = _attention_call(
            functools.partial(_diff_attn_kernel, lam_init=lam_init), b_qkv, bias_b,
            [row(lambda_q1[layer]), row(lambda_k1[layer]), row(lambda_q2[layer]),
             row(lambda_k2[layer]), subln_g[layer].reshape(-1, 1)],
            seq, n_pairs=B_HEADS, q_off=0, k_off=B_HEADS, v_off=2 * B_HEADS, name="diff_attn")
        h = _mix_ffn(h, o_a, o_b, o_c, w_out, row(norm_ffn_g[layer]), w_up, conv_ffn_w[layer], w_down,
                     row(final_g), layer, seq, final_norm=(layer == depth - 1))
    return h.reshape(batch, seq, d)
```

```python
import functools
import math

import numpy as np
import jax
import jax.numpy as jnp
from jax import lax
from jax.experimental import pallas as pl
from jax.experimental.pallas import tpu as pltpu

D_MODEL = 1024
HEAD_DIM = 64
A_HEADS = 4
A_WIDTH = 256
B_HEADS = 4
B_QK = 512
B_V = 512
C_WIDTH = 256
IN_COLS = 3072
DILATED_PATTERNS = ((128, 1), (512, 4), (2048, 16))
CONV_WIDTH = 3
D_FF = 2816
ROPE_THETA = 500000.0
ROPE_DIMS = HEAD_DIM // 4
NORM_EPS = 1e-6
SUBLN_EPS = 1e-5

LANES = 128
CONV_HALO_ROWS = 8
MASK_BIAS = -1e30
LOG2_E = math.log2(math.e)
VMEM_LIMIT_BYTES = 56 * 1024 * 1024

F32 = jnp.float32
BF16 = jnp.bfloat16


def _rms_norm(x, g, eps):
    y = x * lax.rsqrt(jnp.mean(x * x, axis=-1, keepdims=True) + eps)
    return y * g


def _causal_conv3(p, prev, w):
    rows = lax.broadcasted_iota(jnp.int32, p.shape, 0)
    last = prev[CONV_HALO_ROWS - 1:CONV_HALO_ROWS, :]
    last2 = prev[CONV_HALO_ROWS - 2:CONV_HALO_ROWS - 1, :]
    p1 = jnp.where(rows == 0, last, pltpu.roll(p, 1, axis=0))
    p2 = jnp.where(rows == 0, last2, jnp.where(rows == 1, last, pltpu.roll(p, 2, axis=0)))
    return w[0:1, :] * p2 + w[1:2, :] * p1 + w[2:3, :] * p


ROPE_TOKENS_PER_ROW = LANES // ROPE_DIMS


def _rope_table_kernel(pos_ref, inv_ref, o_ref):
    ang = pos_ref[...].astype(F32) * inv_ref[...]
    lane = lax.broadcasted_iota(jnp.int32, ang.shape, 1) % ROPE_DIMS
    o_ref[...] = jnp.where(lane < ROPE_DIMS // 2, jnp.cos(ang), jnp.sin(ang))


def _rope_tables(pos):
    t = pos.shape[0]
    half = ROPE_DIMS // 2
    inv = ROPE_THETA ** (-(jnp.arange(half, dtype=F32) * 2.0 / ROPE_DIMS))
    inv_lane = jnp.tile(inv, LANES // half)[None, :]
    rows = t // ROPE_TOKENS_PER_ROW
    pos_rep = jnp.repeat(pos.reshape(rows, ROPE_TOKENS_PER_ROW), ROPE_DIMS, axis=1)
    compact = pl.pallas_call(
        _rope_table_kernel,
        grid=(1,),
        in_specs=[pl.BlockSpec((rows, LANES), lambda i: (0, 0)),
                  pl.BlockSpec((1, LANES), lambda i: (0, 0))],
        out_specs=pl.BlockSpec((rows, LANES), lambda i: (0, 0)),
        out_shape=jax.ShapeDtypeStruct((rows, LANES), F32),
        name="rope_tables",
    )(pos_rep, inv_lane)
    return compact


IN_ROWS = 512


def _rope_lane_tables(cs):
    half = ROPE_DIMS // 2
    groups = cs.shape[0]
    shape = (groups * ROPE_TOKENS_PER_ROW, LANES)
    spread = jnp.broadcast_to(cs[:, None, :], (groups, ROPE_TOKENS_PER_ROW, LANES)).reshape(shape)
    token = lax.broadcasted_iota(jnp.int32, shape, 0) % ROPE_TOKENS_PER_ROW
    lane = lax.broadcasted_iota(jnp.int32, shape, 1) % HEAD_DIM
    cos_idx = token * ROPE_DIMS + lane % half
    cos = jnp.take_along_axis(spread, cos_idx, axis=1)
    sin = jnp.take_along_axis(spread, cos_idx + half, axis=1)
    return (jnp.where(lane < ROPE_DIMS, cos, 1.0),
            jnp.where((lane >= half) & (lane < ROPE_DIMS), sin, 0.0),
            jnp.where(lane < half, -sin, 0.0))


def _in_proj_kernel(h_ref, g_ref, w_hbm, cs_ref, cw_ref,
                    a_ref, b_ref, oc_ref, hn_ref, carry_ref, w_ref, w_stage, w_sem,
                    *, layer, tiles_per_seq):
    i = pl.program_id(0)

    @pl.when(i == 0)
    def _():
        _load_weight_bf16(w_hbm, layer, w_ref, w_stage, w_sem)

    hn_ref[...] = _rms_norm(h_ref[...], g_ref[...], NORM_EPS).astype(BF16)

    cos, sin_up, sin_dn = _rope_lane_tables(cs_ref[...])
    half = ROPE_DIMS // 2

    def proj(c0, width):
        return jnp.dot(hn_ref[...], w_ref[:, c0:c0 + width], preferred_element_type=F32)

    def rope(y):
        return (y * cos + pltpu.roll(y, half, axis=1) * sin_up
                + pltpu.roll(y, LANES - half, axis=1) * sin_dn)

    def rope_cols(c0, width, scale, out_ref, o0):
        y = proj(c0, width)
        for k in range(width // LANES):
            r = rope(y[:, k * LANES:(k + 1) * LANES])
            if scale != 1.0:
                r = r * scale
            out_ref[:, o0 + k * LANES:o0 + (k + 1) * LANES] = r.astype(BF16)

    scale = HEAD_DIM ** -0.5 * LOG2_E
    rope_cols(0, A_WIDTH, scale, a_ref, 0)
    rope_cols(A_WIDTH, A_WIDTH, 1.0, a_ref, A_WIDTH)
    a_ref[:, 2 * A_WIDTH:3 * A_WIDTH] = proj(2 * A_WIDTH, A_WIDTH).astype(BF16)
    b0 = 3 * A_WIDTH
    rope_cols(b0, B_QK, scale, b_ref, 0)
    rope_cols(b0 + B_QK, B_QK, 1.0, b_ref, B_QK)
    b_ref[:, 2 * B_QK:2 * B_QK + B_V] = proj(b0 + 2 * B_QK, B_V).astype(BF16)

    c0 = b0 + 2 * B_QK + B_V
    gate = proj(c0, C_WIDTH)
    prod = proj(c0 + C_WIDTH, C_WIDTH) * proj(c0 + 2 * C_WIDTH, C_WIDTH)

    @pl.when(i % tiles_per_seq == 0)
    def _():
        carry_ref[...] = jnp.zeros_like(carry_ref)

    prev = carry_ref[...]
    carry_ref[...] = prod[prod.shape[0] - CONV_HALO_ROWS:, :]
    oc_ref[...] = (gate * _causal_conv3(prod, prev, cw_ref[...])).astype(BF16)


WEIGHT_STAGE_BYTES = 1024 * 1024


def _weight_scratch(w):
    rows, cols = w.shape[1:]
    chunk = rows
    while chunk * cols * 4 > WEIGHT_STAGE_BYTES or rows % chunk:
        chunk //= 2
    return [pltpu.VMEM((rows, cols), BF16), pltpu.VMEM((2, chunk, cols), F32),
            pltpu.SemaphoreType.DMA((2,))]


def _load_weight_bf16(w_hbm, layer, dst_ref, stage_ref, sem):
    chunk = stage_ref.shape[1]
    n_chunks = dst_ref.shape[0] // chunk

    def copy(k, slot):
        r0 = pl.multiple_of(k * chunk, chunk)
        return pltpu.make_async_copy(w_hbm.at[layer, pl.ds(r0, chunk), :], stage_ref.at[slot],
                                     sem.at[slot])

    copy(0, 0).start()

    def body(k, carry):
        slot = k % 2

        @pl.when(k + 1 < n_chunks)
        def _():
            copy(k + 1, 1 - slot).start()

        copy(k, slot).wait()
        r0 = pl.multiple_of(k * chunk, chunk)
        dst_ref[pl.ds(r0, chunk), :] = stage_ref[slot].astype(BF16)
        return carry

    lax.fori_loop(0, n_chunks, body, 0)


def _in_proj(h, g, w, layer, rope_cs, conv_w, seq):
    t = h.shape[0]
    rows = IN_ROWS
    row_spec = lambda width: pl.BlockSpec((rows, width), lambda i: (i, 0))
    full = lambda shape: pl.BlockSpec(shape, lambda i: (0, 0))
    return pl.pallas_call(
        functools.partial(_in_proj_kernel, layer=layer, tiles_per_seq=seq // rows),
        grid=(t // rows,),
        in_specs=[row_spec(D_MODEL), full((1, D_MODEL)), pl.BlockSpec(memory_space=pl.ANY),
                  pl.BlockSpec((rows // ROPE_TOKENS_PER_ROW, LANES), lambda i: (i, 0)),
                  full((CONV_WIDTH, C_WIDTH))],
        out_specs=[row_spec(3 * A_WIDTH), row_spec(2 * B_QK + B_V), row_spec(C_WIDTH)],
        out_shape=[jax.ShapeDtypeStruct((t, 3 * A_WIDTH), BF16),
                   jax.ShapeDtypeStruct((t, 2 * B_QK + B_V), BF16),
                   jax.ShapeDtypeStruct((t, C_WIDTH), BF16)],
        scratch_shapes=[pltpu.VMEM((rows, D_MODEL), BF16),
                        pltpu.VMEM((CONV_HALO_ROWS, C_WIDTH), F32)] + _weight_scratch(w),
        compiler_params=pltpu.CompilerParams(dimension_semantics=("arbitrary",),
                                             vmem_limit_bytes=VMEM_LIMIT_BYTES),
        name="in_proj",
    )(h, g, w, rope_cs, conv_w)


ATT_ROWS = 256
SCORE_SLOTS = 2
ONES_ROWS = 16


def _dilated_bias_table(seq):
    u = np.arange(seq)[:, None]
    r = np.arange(ATT_ROWS)[None, :]
    delta = r + seq - ATT_ROWS - u
    count = np.zeros(delta.shape, np.int64)
    for window, dil in DILATED_PATTERNS:
        count += (delta >= 0) & (delta <= window) & (delta % dil == 0)
    return np.where(count > 0, np.log2(np.maximum(count, 1)), MASK_BIAS).astype(np.float32)


def _causal_bias_table():
    key = np.arange(ATT_ROWS)[:, None]
    query = np.arange(ATT_ROWS)[None, :]
    return np.where(key <= query, 0.0, MASK_BIAS).astype(np.float32)


def _pair_attention(q_ref, k_ref, v_ref, vt_ref, s_refs, block_bias, emit):
    rows = ATT_ROWS
    n_chunks = q_ref.shape[0] // rows
    vt_ref[0:LANES, :] = v_ref[...].astype(F32).T.astype(BF16)
    vt_ref[LANES:, :] = jnp.ones((ONES_ROWS, vt_ref.shape[1]), BF16)
    lane = lax.broadcasted_iota(jnp.int32, (rows, LANES), 1)

    def stacked_queries(c):
        q = q_ref[c * rows:(c + 1) * rows, :]
        zero = jnp.zeros_like(q)
        return jnp.concatenate([jnp.where(lane < HEAD_DIM, q, zero),
                                jnp.where(lane >= HEAD_DIM, q, zero)], axis=0)

    half = rows // 2
    nt_dims = (((1,), (1,)), ((), ()))

    def late_lanes(x):
        return jnp.concatenate([x[:, half:rows], x[:, rows + half:]], axis=1)

    def late_lanes_only(x, fill):
        pad = jnp.full((x.shape[0], half), fill, x.dtype)
        return jnp.concatenate([pad, x[:, :half], pad, x[:, half:]], axis=1)

    def sublane_max(s):
        return jnp.max(s.reshape(s.shape[0] // 8, 8, s.shape[1]), axis=0)

    def score_block(c, j, qs, col_max):
        slot = s_refs[c % SCORE_SLOTS]
        bias = block_bias(c, j)
        if j < c:
            s = lax.dot_general(k_ref[j * rows:(j + 1) * rows, :], qs, nt_dims,
                                preferred_element_type=F32)
            if bias is not None:
                s = s + jnp.concatenate([bias, bias], axis=1)
            slot[j * rows:(j + 1) * rows, :] = s
            blk_max = sublane_max(s)
        else:
            mid = j * rows + half
            s = lax.dot_general(k_ref[j * rows:mid, :], qs, nt_dims, preferred_element_type=F32)
            s = s + jnp.concatenate([bias[:half], bias[:half]], axis=1)
            slot[j * rows:mid, :] = s
            qs_late = jnp.concatenate([qs[half:rows], qs[rows + half:]], axis=0)
            s_late = lax.dot_general(k_ref[mid:mid + half, :], qs_late, nt_dims,
                                     preferred_element_type=F32)
            s_late = s_late + jnp.concatenate([bias[half:, half:], bias[half:, half:]], axis=1)
            slot[mid:mid + half, 0:rows] = s_late
            blk_max = jnp.maximum(sublane_max(s), late_lanes_only(sublane_max(s_late), MASK_BIAS))
        return blk_max if col_max is None else jnp.maximum(col_max, blk_max)

    def value_block(c, j, m, acc):
        slot = s_refs[c % SCORE_SLOTS]
        if j < c:
            p = jnp.exp2(slot[j * rows:(j + 1) * rows, :] - m).astype(BF16)
            part = jnp.dot(vt_ref[:, j * rows:(j + 1) * rows], p, preferred_element_type=F32)
            return part if acc is None else acc + part
        mid = j * rows + half
        p = jnp.exp2(slot[j * rows:mid, :] - m).astype(BF16)
        part = jnp.dot(vt_ref[:, j * rows:mid], p, preferred_element_type=F32)
        acc = part if acc is None else acc + part
        p_late = jnp.exp2(slot[mid:mid + half, 0:rows] - late_lanes(m)).astype(BF16)
        part = jnp.dot(vt_ref[:, mid:mid + half], p_late, preferred_element_type=F32)
        return acc + late_lanes_only(part, 0.0)

    qs = stacked_queries(0)
    col_max = score_block(0, 0, qs, None)
    acc = None
    for c in range(n_chunks):
        m = jnp.max(col_max, axis=0, keepdims=True)
        nxt = c + 1
        if nxt < n_chunks:
            qs = stacked_queries(nxt)
        acc = None
        col_max = None
        for j in range(nxt + 1):
            if nxt < n_chunks:
                col_max = score_block(nxt, j, qs, col_max)
            if j <= c:
                acc = value_block(c, j, m, acc)
        emit(c, acc[0:LANES], acc[LANES:LANES + 1])


def _dilated_attn_kernel(q_ref, k_ref, v_ref, bias_ref, o_ref, vt_ref, *s_refs):
    rows = ATT_ROWS
    seq = q_ref.shape[0]

    def block_bias(c, j):
        first = seq - (c + 1 - j) * rows
        return bias_ref[first:first + rows, :]

    def emit(c, o_t, l):
        o_t = o_t * (1.0 / l)
        o = jnp.concatenate([o_t[:HEAD_DIM, :rows], o_t[HEAD_DIM:, rows:]], axis=0)
        o_ref[c * rows:(c + 1) * rows, :] = o.T.astype(o_ref.dtype)

    _pair_attention(q_ref, k_ref, v_ref, vt_ref, s_refs, block_bias, emit)


def _diff_attn_kernel(q_ref, k_ref, v_ref, bias_ref, lq1_ref, lk1_ref, lq2_ref, lk2_ref, g_ref,
                      o_ref, vt_ref, *s_refs, lam_init):
    rows = ATT_ROWS
    lam = (jnp.exp(jnp.sum(lq1_ref[...] * lk1_ref[...], axis=1, keepdims=True))
           - jnp.exp(jnp.sum(lq2_ref[...] * lk2_ref[...], axis=1, keepdims=True)) + lam_init)
    gain = g_ref[...]

    def block_bias(c, j):
        return bias_ref[...] if j == c else None

    def emit(c, o_t, l):
        o_t = o_t * (1.0 / l)
        o = o_t[:, :rows] - lam * o_t[:, rows:]
        o = o * lax.rsqrt(jnp.mean(o * o, axis=0, keepdims=True) + SUBLN_EPS) * gain
        o_ref[c * rows:(c + 1) * rows, :] = (o * (1.0 - lam_init)).T.astype(o_ref.dtype)

    _pair_attention(q_ref, k_ref, v_ref, vt_ref, s_refs, block_bias, emit)


def _attention_call(body, qkv, bias, extra, seq, n_pairs, q_off, k_off, v_off, name):
    t = qkv.shape[0]
    batch = t // seq
    small = lambda a: pl.BlockSpec(a.shape, lambda b, p: (0,) * a.ndim)
    col = lambda off: pl.BlockSpec((seq, LANES), lambda b, p: (b, off + p))
    return pl.pallas_call(
        body,
        grid=(batch, n_pairs),
        in_specs=[col(q_off), col(k_off), col(v_off), small(bias)] + [small(a) for a in extra],
        out_specs=col(0),
        out_shape=jax.ShapeDtypeStruct((t, n_pairs * LANES), BF16),
        scratch_shapes=([pltpu.VMEM((LANES + ONES_ROWS, seq), BF16)]
                        + [pltpu.VMEM((seq, 2 * ATT_ROWS), F32)] * SCORE_SLOTS),
        compiler_params=pltpu.CompilerParams(dimension_semantics=("arbitrary", "arbitrary"),
                                             vmem_limit_bytes=VMEM_LIMIT_BYTES),
        name=name,
    )(qkv, qkv, qkv, bias, *extra)


FFN_ROWS = 512
FFN_CHUNK = 512


def _mix_ffn_kernel(h_ref, oa_ref, ob_ref, oc_ref, wo_hbm, g_ref, wup_hbm, cw_ref, wd_hbm, fg_ref,
                    o_ref, hn_ref, act_ref, carry_ref, wo_ref, wo_stage, wo_sem, wup_ref, wup_stage,
                    wup_sem, wd_ref, wd_stage, wd_sem, *, layer, tiles_per_seq, final_norm):
    i = pl.program_id(0)

    @pl.when(i == 0)
    def _():
        _load_weight_bf16(wo_hbm, layer, wo_ref, wo_stage, wo_sem)
        _load_weight_bf16(wup_hbm, layer, wup_ref, wup_stage, wup_sem)
        _load_weight_bf16(wd_hbm, layer, wd_ref, wd_stage, wd_sem)

    mix = jnp.dot(oa_ref[...], wo_ref[0:A_WIDTH, :], preferred_element_type=F32)
    mix += jnp.dot(ob_ref[...], wo_ref[A_WIDTH:A_WIDTH + B_V, :], preferred_element_type=F32)
    mix += jnp.dot(oc_ref[...], wo_ref[A_WIDTH + B_V:, :], preferred_element_type=F32)
    h_mid = h_ref[...] + mix
    o_ref[...] = h_mid
    hn_ref[...] = _rms_norm(h_mid, g_ref[...], NORM_EPS).astype(BF16)

    @pl.when(i % tiles_per_seq == 0)
    def _():
        carry_ref[...] = jnp.zeros_like(carry_ref)

    def gate_up(cols):
        up_cols = slice(D_FF + cols.start, D_FF + cols.stop)
        hn = hn_ref[...]
        return (jnp.dot(hn, wup_ref[:, cols], preferred_element_type=F32),
                jnp.dot(hn, wup_ref[:, up_cols], preferred_element_type=F32))

    def activate(cols, gate, up):
        prev = carry_ref[:, cols]
        carry_ref[:, cols] = gate[gate.shape[0] - CONV_HALO_ROWS:, :]
        gc = _causal_conv3(gate, prev, cw_ref[:, cols])
        act_ref[:, cols] = (gc / (1.0 + jnp.exp(-gc)) * up).astype(BF16)

    chunks = [slice(c0, min(c0 + FFN_CHUNK, D_FF)) for c0 in range(0, D_FF, FFN_CHUNK)]
    pending = gate_up(chunks[0])
    for c, cols in enumerate(chunks):
        nxt = gate_up(chunks[c + 1]) if c + 1 < len(chunks) else None
        activate(cols, *pending)
        pending = nxt

    out = o_ref[...] + jnp.dot(act_ref[...], wd_ref[...], preferred_element_type=F32)
    if final_norm:
        out = _rms_norm(out, fg_ref[...], NORM_EPS)
    o_ref[...] = out


def _mix_ffn(h, o_a, o_b, o_c, w_out, g, w_up, conv_w, w_down, final_g, layer, seq, final_norm):
    t = h.shape[0]
    rows = FFN_ROWS
    row_spec = lambda width: pl.BlockSpec((rows, width), lambda i: (i, 0))
    full = lambda shape: pl.BlockSpec(shape, lambda i: (0, 0))
    hbm = pl.BlockSpec(memory_space=pl.ANY)
    return pl.pallas_call(
        functools.partial(_mix_ffn_kernel, layer=layer, tiles_per_seq=seq // rows,
                          final_norm=final_norm),
        grid=(t // rows,),
        in_specs=[row_spec(D_MODEL), row_spec(A_WIDTH), row_spec(B_V), row_spec(C_WIDTH),
                  hbm, full((1, D_MODEL)), hbm, full((CONV_WIDTH, D_FF)), hbm, full((1, D_MODEL))],
        out_specs=row_spec(D_MODEL),
        out_shape=jax.ShapeDtypeStruct((t, D_MODEL), F32),
        scratch_shapes=([pltpu.VMEM((rows, D_MODEL), BF16),
                         pltpu.VMEM((rows, D_FF), BF16),
                         pltpu.VMEM((CONV_HALO_ROWS, D_FF), F32)]
                        + _weight_scratch(w_out) + _weight_scratch(w_up) + _weight_scratch(w_down)),
        compiler_params=pltpu.CompilerParams(dimension_semantics=("arbitrary",),
                                             vmem_limit_bytes=VMEM_LIMIT_BYTES),
        name="mix_ffn",
    )(h, o_a, o_b, o_c, w_out, g, w_up, conv_w, w_down, final_g)


def kernel(x, positions, norm_mix_g, w_in, lambda_q1, lambda_k1, lambda_q2, lambda_k2, subln_g,
           conv_mix_w, w_out, norm_ffn_g, w_up, conv_ffn_w, w_down, final_g):
    batch, seq, d = x.shape
    depth = w_in.shape[0]
    assert d == D_MODEL and seq % ATT_ROWS == 0 and seq % IN_ROWS == 0 and seq % FFN_ROWS == 0
    t = batch * seq
    h = x.reshape(t, d)
    tables = _rope_tables(positions.reshape(t))
    bias_a = jnp.asarray(_dilated_bias_table(seq))
    bias_b = jnp.asarray(_causal_bias_table())
    row = lambda v: v.reshape(1, -1)

    for layer in range(depth):
        a_qkv, b_qkv, o_c = _in_proj(h, row(norm_mix_g[layer]), w_in, layer, tables,
                                     conv_mix_w[layer], seq)
        o_a = _attention_call(_dilated_attn_kernel, a_qkv, bias_a, [], seq,
                              n_pairs=A_HEADS // 2, q_off=0, k_off=2, v_off=4, name="dilated_attn")
        lam_init = 0.8 - 0.6 * math.exp(-0.3 * layer)
        o_b = _attention_call(
            functools.partial(_diff_attn_kernel, lam_init=lam_init), b_qkv, bias_b,
            [row(lambda_q1[layer]), row(lambda_k1[layer]), row(lambda_q2[layer]),
             row(lambda_k2[layer]), subln_g[layer].reshape(-1, 1)],
            seq, n_pairs=B_HEADS, q_off=0, k_off=B_HEADS, v_off=2 * B_HEADS, name="diff_attn")
        h = _mix_ffn(h, o_a, o_b, o_c, w_out, row(norm_ffn_g[layer]), w_up, conv_ffn_w[layer], w_down,
                     row(final_g), layer, seq, final_norm=(layer == depth - 1))
    return h.reshape(batch, seq, d)
```

```python
import functools
import math

import numpy as np
import jax
import jax.numpy as jnp
from jax import lax
from jax.experimental import pallas as pl
from jax.experimental.pallas import tpu as pltpu

D_MODEL = 1024
HEAD_DIM = 64
A_HEADS = 4
A_WIDTH = 256
B_HEADS = 4
B_QK = 512
B_V = 512
C_WIDTH = 256
IN_COLS = 3072
DILATED_PATTERNS = ((128, 1), (512, 4), (2048, 16))
CONV_WIDTH = 3
D_FF = 2816
ROPE_THETA = 500000.0
ROPE_DIMS = HEAD_DIM // 4
NORM_EPS = 1e-6
SUBLN_EPS = 1e-5

LANES = 128
CONV_HALO_ROWS = 8
MASK_BIAS = -1e30
LOG2_E = math.log2(math.e)
VMEM_LIMIT_BYTES = 56 * 1024 * 1024

F32 = jnp.float32
BF16 = jnp.bfloat16


def _rms_norm(x, g, eps):
    y = x * lax.rsqrt(jnp.mean(x * x, axis=-1, keepdims=True) + eps)
    return y * g


def _causal_conv3(p, prev, w):
    rows = lax.broadcasted_iota(jnp.int32, p.shape, 0)
    last = prev[CONV_HALO_ROWS - 1:CONV_HALO_ROWS, :]
    last2 = prev[CONV_HALO_ROWS - 2:CONV_HALO_ROWS - 1, :]
    p1 = jnp.where(rows == 0, last, pltpu.roll(p, 1, axis=0))
    p2 = jnp.where(rows == 0, last2, jnp.where(rows == 1, last, pltpu.roll(p, 2, axis=0)))
    return w[0:1, :] * p2 + w[1:2, :] * p1 + w[2:3, :] * p


ROPE_TOKENS_PER_ROW = LANES // ROPE_DIMS


def _rope_table_kernel(pos_ref, inv_ref, o_ref):
    ang = pos_ref[...].astype(F32) * inv_ref[...]
    lane = lax.broadcasted_iota(jnp.int32, ang.shape, 1) % ROPE_DIMS
    o_ref[...] = jnp.where(lane < ROPE_DIMS // 2, jnp.cos(ang), jnp.sin(ang))


def _rope_tables(pos):
    t = pos.shape[0]
    half = ROPE_DIMS // 2
    inv = ROPE_THETA ** (-(jnp.arange(half, dtype=F32) * 2.0 / ROPE_DIMS))
    inv_lane = jnp.tile(inv, LANES // half)[None, :]
    rows = t // ROPE_TOKENS_PER_ROW
    pos_rep = jnp.repeat(pos.reshape(rows, ROPE_TOKENS_PER_ROW), ROPE_DIMS, axis=1)
    compact = pl.pallas_call(
        _rope_table_kernel,
        grid=(1,),
        in_specs=[pl.BlockSpec((rows, LANES), lambda i: (0, 0)),
                  pl.BlockSpec((1, LANES), lambda i: (0, 0))],
        out_specs=pl.BlockSpec((rows, LANES), lambda i: (0, 0)),
        out_shape=jax.ShapeDtypeStruct((rows, LANES), F32),
        name="rope_tables",
    )(pos_rep, inv_lane)
    return compact


IN_ROWS = 512


def _rope_lane_tables(cs):
    half = ROPE_DIMS // 2
    groups = cs.shape[0]
    shape = (groups * ROPE_TOKENS_PER_ROW, LANES)
    spread = jnp.broadcast_to(cs[:, None, :], (groups, ROPE_TOKENS_PER_ROW, LANES)).reshape(shape)
    token = lax.broadcasted_iota(jnp.int32, shape, 0) % ROPE_TOKENS_PER_ROW
    lane = lax.broadcasted_iota(jnp.int32, shape, 1) % HEAD_DIM
    cos_idx = token * ROPE_DIMS + lane % half
    cos = jnp.take_along_axis(spread, cos_idx, axis=1)
    sin = jnp.take_along_axis(spread, cos_idx + half, axis=1)
    return (jnp.where(lane < ROPE_DIMS, cos, 1.0),
            jnp.where((lane >= half) & (lane < ROPE_DIMS), sin, 0.0),
            jnp.where(lane < half, -sin, 0.0))


def _in_proj_kernel(h_ref, g_ref, w_hbm, cs_ref, cw_ref,
                    a_ref, b_ref, oc_ref, hn_ref, carry_ref, w_ref, w_stage, w_sem,
                    *, layer, tiles_per_seq):
    i = pl.program_id(0)

    @pl.when(i == 0)
    def _():
        _load_weights_bf16([(w_hbm, w_ref)], layer, w_stage, w_sem)

    hn_ref[...] = _rms_norm(h_ref[...], g_ref[...], NORM_EPS).astype(BF16)

    cos, sin_up, sin_dn = _rope_lane_tables(cs_ref[...])
    half = ROPE_DIMS // 2

    def proj(c0, width):
        return jnp.dot(hn_ref[...], w_ref[:, c0:c0 + width], preferred_element_type=F32)

    def rope(y):
        return (y * cos + pltpu.roll(y, half, axis=1) * sin_up
                + pltpu.roll(y, LANES - half, axis=1) * sin_dn)

    def rope_cols(c0, width, scale, out_ref, o0):
        y = proj(c0, width)
        for k in range(width // LANES):
            r = rope(y[:, k * LANES:(k + 1) * LANES])
            if scale != 1.0:
                r = r * scale
            out_ref[:, o0 + k * LANES:o0 + (k + 1) * LANES] = r.astype(BF16)

    scale = HEAD_DIM ** -0.5 * LOG2_E
    rope_cols(0, A_WIDTH, scale, a_ref, 0)
    rope_cols(A_WIDTH, A_WIDTH, 1.0, a_ref, A_WIDTH)
    a_ref[:, 2 * A_WIDTH:3 * A_WIDTH] = proj(2 * A_WIDTH, A_WIDTH).astype(BF16)
    b0 = 3 * A_WIDTH
    rope_cols(b0, B_QK, scale, b_ref, 0)
    rope_cols(b0 + B_QK, B_QK, 1.0, b_ref, B_QK)
    b_ref[:, 2 * B_QK:2 * B_QK + B_V] = proj(b0 + 2 * B_QK, B_V).astype(BF16)

    c0 = b0 + 2 * B_QK + B_V
    gate = proj(c0, C_WIDTH)
    prod = proj(c0 + C_WIDTH, C_WIDTH) * proj(c0 + 2 * C_WIDTH, C_WIDTH)

    @pl.when(i % tiles_per_seq == 0)
    def _():
        carry_ref[...] = jnp.zeros_like(carry_ref)

    prev = carry_ref[...]
    carry_ref[...] = prod[prod.shape[0] - CONV_HALO_ROWS:, :]
    oc_ref[...] = (gate * _causal_conv3(prod, prev, cw_ref[...])).astype(BF16)


WEIGHT_CHUNK = (256, 512)
WEIGHT_RING = 6


def _weight_ring_scratch():
    return [pltpu.VMEM((WEIGHT_RING,) + WEIGHT_CHUNK, F32), pltpu.SemaphoreType.DMA((WEIGHT_RING,))]


def _load_weights_bf16(jobs, layer, stage_ref, sem):
    cr, cc = WEIGHT_CHUNK
    chunks = [(w, dst, r0, c0) for w, dst in jobs
              for r0 in range(0, dst.shape[0], cr) for c0 in range(0, dst.shape[1], cc)]

    def copy(k):
        w, _, r0, c0 = chunks[k]
        slot = k % WEIGHT_RING
        return pltpu.make_async_copy(w.at[layer, r0:r0 + cr, c0:c0 + cc], stage_ref.at[slot],
                                     sem.at[slot])

    for k in range(min(WEIGHT_RING, len(chunks))):
        copy(k).start()
    for k, (_, dst, r0, c0) in enumerate(chunks):
        copy(k).wait()
        dst[r0:r0 + cr, c0:c0 + cc] = stage_ref[k % WEIGHT_RING].astype(BF16)
        if k + WEIGHT_RING < len(chunks):
            copy(k + WEIGHT_RING).start()


def _in_proj(h, g, w, layer, rope_cs, conv_w, seq):
    t = h.shape[0]
    rows = IN_ROWS
    row_spec = lambda width: pl.BlockSpec((rows, width), lambda i: (i, 0))
    full = lambda shape: pl.BlockSpec(shape, lambda i: (0, 0))
    return pl.pallas_call(
        functools.partial(_in_proj_kernel, layer=layer, tiles_per_seq=seq // rows),
        grid=(t // rows,),
        in_specs=[row_spec(D_MODEL), full((1, D_MODEL)), pl.BlockSpec(memory_space=pl.ANY),
                  pl.BlockSpec((rows // ROPE_TOKENS_PER_ROW, LANES), lambda i: (i, 0)),
                  full((CONV_WIDTH, C_WIDTH))],
        out_specs=[row_spec(3 * A_WIDTH), row_spec(2 * B_QK + B_V), row_spec(C_WIDTH)],
        out_shape=[jax.ShapeDtypeStruct((t, 3 * A_WIDTH), BF16),
                   jax.ShapeDtypeStruct((t, 2 * B_QK + B_V), BF16),
                   jax.ShapeDtypeStruct((t, C_WIDTH), BF16)],
        scratch_shapes=[pltpu.VMEM((rows, D_MODEL), BF16),
                        pltpu.VMEM((CONV_HALO_ROWS, C_WIDTH), F32),
                        pltpu.VMEM(w.shape[1:], BF16)] + _weight_ring_scratch(),
        compiler_params=pltpu.CompilerParams(dimension_semantics=("arbitrary",),
                                             vmem_limit_bytes=VMEM_LIMIT_BYTES),
        name="in_proj",
    )(h, g, w, rope_cs, conv_w)


ATT_ROWS = 256
SCORE_SLOTS = 2
ONES_ROWS = 16


def _dilated_bias_table(seq):
    u = np.arange(seq)[:, None]
    r = np.arange(ATT_ROWS)[None, :]
    delta = r + seq - ATT_ROWS - u
    count = np.zeros(delta.shape, np.int64)
    for window, dil in DILATED_PATTERNS:
        count += (delta >= 0) & (delta <= window) & (delta % dil == 0)
    return np.where(count > 0, np.log2(np.maximum(count, 1)), MASK_BIAS).astype(np.float32)


def _causal_bias_table():
    key = np.arange(ATT_ROWS)[:, None]
    query = np.arange(ATT_ROWS)[None, :]
    return np.where(key <= query, 0.0, MASK_BIAS).astype(np.float32)


def _pair_attention(q_ref, k_ref, v_ref, vt_ref, s_refs, block_bias, emit):
    rows = ATT_ROWS
    n_chunks = q_ref.shape[0] // rows
    vt_ref[0:LANES, :] = v_ref[...].astype(F32).T.astype(BF16)
    vt_ref[LANES:, :] = jnp.ones((ONES_ROWS, vt_ref.shape[1]), BF16)
    lane = lax.broadcasted_iota(jnp.int32, (rows, LANES), 1)

    def stacked_queries(c):
        q = q_ref[c * rows:(c + 1) * rows, :]
        zero = jnp.zeros_like(q)
        return jnp.concatenate([jnp.where(lane < HEAD_DIM, q, zero),
                                jnp.where(lane >= HEAD_DIM, q, zero)], axis=0)

    half = rows // 2
    nt_dims = (((1,), (1,)), ((), ()))

    def late_lanes(x):
        return jnp.concatenate([x[:, half:rows], x[:, rows + half:]], axis=1)

    def late_lanes_only(x, fill):
        pad = jnp.full((x.shape[0], half), fill, x.dtype)
        return jnp.concatenate([pad, x[:, :half], pad, x[:, half:]], axis=1)

    def sublane_max(s):
        return jnp.max(s.reshape(s.shape[0] // 8, 8, s.shape[1]), axis=0)

    def score_block(c, j, qs, col_max):
        slot = s_refs[c % SCORE_SLOTS]
        bias = block_bias(c, j)
        if j < c:
            s = lax.dot_general(k_ref[j * rows:(j + 1) * rows, :], qs, nt_dims,
                                preferred_element_type=F32)
            if bias is not None:
                s = s + jnp.concatenate([bias, bias], axis=1)
            slot[j * rows:(j + 1) * rows, :] = s
            blk_max = sublane_max(s)
        else:
            mid = j * rows + half
            s = lax.dot_general(k_ref[j * rows:mid, :], qs, nt_dims, preferred_element_type=F32)
            s = s + jnp.concatenate([bias[:half], bias[:half]], axis=1)
            slot[j * rows:mid, :] = s
            qs_late = jnp.concatenate([qs[half:rows], qs[rows + half:]], axis=0)
            s_late = lax.dot_general(k_ref[mid:mid + half, :], qs_late, nt_dims,
                                     preferred_element_type=F32)
            s_late = s_late + jnp.concatenate([bias[half:, half:], bias[half:, half:]], axis=1)
            slot[mid:mid + half, 0:rows] = s_late
            blk_max = jnp.maximum(sublane_max(s), late_lanes_only(sublane_max(s_late), MASK_BIAS))
        return blk_max if col_max is None else jnp.maximum(col_max, blk_max)

    def value_block(c, j, m, acc):
        slot = s_refs[c % SCORE_SLOTS]
        if j < c:
            p = jnp.exp2(slot[j * rows:(j + 1) * rows, :] - m).astype(BF16)
            part = jnp.dot(vt_ref[:, j * rows:(j + 1) * rows], p, preferred_element_type=F32)
            return part if acc is None else acc + part
        mid = j * rows + half
        p = jnp.exp2(slot[j * rows:mid, :] - m).astype(BF16)
        part = jnp.dot(vt_ref[:, j * rows:mid], p, preferred_element_type=F32)
        acc = part if acc is None else acc + part
        p_late = jnp.exp2(slot[mid:mid + half, 0:rows] - late_lanes(m)).astype(BF16)
        part = jnp.dot(vt_ref[:, mid:mid + half], p_late, preferred_element_type=F32)
        return acc + late_lanes_only(part, 0.0)

    qs = stacked_queries(0)
    col_max = score_block(0, 0, qs, None)
    acc = None
    for c in range(n_chunks):
        m = jnp.max(col_max, axis=0, keepdims=True)
        nxt = c + 1
        if nxt < n_chunks:
            qs = stacked_queries(nxt)
        acc = None
        col_max = None
        for j in range(nxt + 1):
            if nxt < n_chunks:
                col_max = score_block(nxt, j, qs, col_max)
            if j <= c:
                acc = value_block(c, j, m, acc)
        emit(c, acc[0:LANES], acc[LANES:LANES + 1])


def _dilated_attn_kernel(q_ref, k_ref, v_ref, bias_ref, o_ref, vt_ref, *s_refs):
    rows = ATT_ROWS
    seq = q_ref.shape[0]

    def block_bias(c, j):
        first = seq - (c + 1 - j) * rows
        return bias_ref[first:first + rows, :]

    def emit(c, o_t, l):
        o_t = o_t * (1.0 / l)
        o = jnp.concatenate([o_t[:HEAD_DIM, :rows], o_t[HEAD_DIM:, rows:]], axis=0)
        o_ref[c * rows:(c + 1) * rows, :] = o.T.astype(o_ref.dtype)

    _pair_attention(q_ref, k_ref, v_ref, vt_ref, s_refs, block_bias, emit)


def _diff_attn_kernel(q_ref, k_ref, v_ref, bias_ref, lq1_ref, lk1_ref, lq2_ref, lk2_ref, g_ref,
                      o_ref, vt_ref, *s_refs, lam_init):
    rows = ATT_ROWS
    lam = (jnp.exp(jnp.sum(lq1_ref[...] * lk1_ref[...], axis=1, keepdims=True))
           - jnp.exp(jnp.sum(lq2_ref[...] * lk2_ref[...], axis=1, keepdims=True)) + lam_init)
    gain = g_ref[...]

    def block_bias(c, j):
        return bias_ref[...] if j == c else None

    def emit(c, o_t, l):
        o_t = o_t * (1.0 / l)
        o = o_t[:, :rows] - lam * o_t[:, rows:]
        o = o * lax.rsqrt(jnp.mean(o * o, axis=0, keepdims=True) + SUBLN_EPS) * gain
        o_ref[c * rows:(c + 1) * rows, :] = (o * (1.0 - lam_init)).T.astype(o_ref.dtype)

    _pair_attention(q_ref, k_ref, v_ref, vt_ref, s_refs, block_bias, emit)


def _attention_call(body, qkv, bias, extra, seq, n_pairs, q_off, k_off, v_off, name):
    t = qkv.shape[0]
    batch = t // seq
    small = lambda a: pl.BlockSpec(a.shape, lambda b, p: (0,) * a.ndim)
    col = lambda off: pl.BlockSpec((seq, LANES), lambda b, p: (b, off + p))
    return pl.pallas_call(
        body,
        grid=(batch, n_pairs),
        in_specs=[col(q_off), col(k_off), col(v_off), small(bias)] + [small(a) for a in extra],
        out_specs=col(0),
        out_shape=jax.ShapeDtypeStruct((t, n_pairs * LANES), BF16),
        scratch_shapes=([pltpu.VMEM((LANES + ONES_ROWS, seq), BF16)]
                        + [pltpu.VMEM((seq, 2 * ATT_ROWS), F32)] * SCORE_SLOTS),
        compiler_params=pltpu.CompilerParams(dimension_semantics=("arbitrary", "arbitrary"),
                                             vmem_limit_bytes=VMEM_LIMIT_BYTES),
        name=name,
    )(qkv, qkv, qkv, bias, *extra)


FFN_ROWS = 512
FFN_CHUNK = 512


def _mix_ffn_kernel(h_ref, oa_ref, ob_ref, oc_ref, wo_hbm, g_ref, wup_hbm, cw_ref, wd_hbm, fg_ref,
                    o_ref, hn_ref, act_ref, carry_ref, wo_ref, wup_ref, wd_ref, w_stage, w_sem,
                    *, layer, tiles_per_seq, final_norm):
    i = pl.program_id(0)

    @pl.when(i == 0)
    def _():
        _load_weights_bf16([(wo_hbm, wo_ref), (wup_hbm, wup_ref), (wd_hbm, wd_ref)], layer,
                           w_stage, w_sem)

    mix = jnp.dot(oa_ref[...], wo_ref[0:A_WIDTH, :], preferred_element_type=F32)
    mix += jnp.dot(ob_ref[...], wo_ref[A_WIDTH:A_WIDTH + B_V, :], preferred_element_type=F32)
    mix += jnp.dot(oc_ref[...], wo_ref[A_WIDTH + B_V:, :], preferred_element_type=F32)
    h_mid = h_ref[...] + mix
    o_ref[...] = h_mid
    hn_ref[...] = _rms_norm(h_mid, g_ref[...], NORM_EPS).astype(BF16)

    @pl.when(i % tiles_per_seq == 0)
    def _():
        carry_ref[...] = jnp.zeros_like(carry_ref)

    def gate_up(cols):
        up_cols = slice(D_FF + cols.start, D_FF + cols.stop)
        hn = hn_ref[...]
        return (jnp.dot(hn, wup_ref[:, cols], preferred_element_type=F32),
                jnp.dot(hn, wup_ref[:, up_cols], preferred_element_type=F32))

    def activate(cols, gate, up):
        prev = carry_ref[:, cols]
        carry_ref[:, cols] = gate[gate.shape[0] - CONV_HALO_ROWS:, :]
        gc = _causal_conv3(gate, prev, cw_ref[:, cols])
        act_ref[:, cols] = (gc / (1.0 + jnp.exp(-gc)) * up).astype(BF16)

    chunks = [slice(c0, min(c0 + FFN_CHUNK, D_FF)) for c0 in range(0, D_FF, FFN_CHUNK)]
    pending = gate_up(chunks[0])
    for c, cols in enumerate(chunks):
        nxt = gate_up(chunks[c + 1]) if c + 1 < len(chunks) else None
        activate(cols, *pending)
        pending = nxt

    out = o_ref[...] + jnp.dot(act_ref[...], wd_ref[...], preferred_element_type=F32)
    if final_norm:
        out = _rms_norm(out, fg_ref[...], NORM_EPS)
    o_ref[...] = out


def _mix_ffn(h, o_a, o_b, o_c, w_out, g, w_up, conv_w, w_down, final_g, layer, seq, final_norm):
    t = h.shape[0]
    rows = FFN_ROWS
    row_spec = lambda width: pl.BlockSpec((rows, width), lambda i: (i, 0))
    full = lambda shape: pl.BlockSpec(shape, lambda i: (0, 0))
    hbm = pl.BlockSpec(memory_space=pl.ANY)
    return pl.pallas_call(
        functools.partial(_mix_ffn_kernel, layer=layer, tiles_per_seq=seq // rows,
                          final_norm=final_norm),
        grid=(t // rows,),
        in_specs=[row_spec(D_MODEL), row_spec(A_WIDTH), row_spec(B_V), row_spec(C_WIDTH),
                  hbm, full((1, D_MODEL)), hbm, full((CONV_WIDTH, D_FF)), hbm, full((1, D_MODEL))],
        out_specs=row_spec(D_MODEL),
        out_shape=jax.ShapeDtypeStruct((t, D_MODEL), F32),
        scratch_shapes=([pltpu.VMEM((rows, D_MODEL), BF16),
                         pltpu.VMEM((rows, D_FF), BF16),
                         pltpu.VMEM((CONV_HALO_ROWS, D_FF), F32)]
                        + [pltpu.VMEM(w.shape[1:], BF16) for w in (w_out, w_up, w_down)]
                        + _weight_ring_scratch()),
        compiler_params=pltpu.CompilerParams(dimension_semantics=("arbitrary",),
                                             vmem_limit_bytes=VMEM_LIMIT_BYTES),
        name="mix_ffn",
    )(h, o_a, o_b, o_c, w_out, g, w_up, conv_w, w_down, final_g)


def kernel(x, positions, norm_mix_g, w_in, lambda_q1, lambda_k1, lambda_q2, lambda_k2, subln_g,
           conv_mix_w, w_out, norm_ffn_g, w_up, conv_ffn_w, w_down, final_g):
    batch, seq, d = x.shape
    depth = w_in.shape[0]
    assert d == D_MODEL and seq % ATT_ROWS == 0 and seq % IN_ROWS == 0 and seq % FFN_ROWS == 0
    t = batch * seq
    h = x.reshape(t, d)
    tables = _rope_tables(positions.reshape(t))
    bias_a = jnp.asarray(_dilated_bias_table(seq))
    bias_b = jnp.asarray(_causal_bias_table())
    row = lambda v: v.reshape(1, -1)

    for layer in range(depth):
        a_qkv, b_qkv, o_c = _in_proj(h, row(norm_mix_g[layer]), w_in, layer, tables,
                                     conv_mix_w[layer], seq)
        o_a = _attention_call(_dilated_attn_kernel, a_qkv, bias_a, [], seq,
                              n_pairs=A_HEADS // 2, q_off=0, k_off=2, v_off=4, name="dilated_attn")
        lam_init = 0.8 - 0.6 * math.exp(-0.3 * layer)
        o_b = _attention_call(
            functools.partial(_diff_attn_kernel, lam_init=lam_init), b_qkv, bias_b,
            [row(lambda_q1[layer]), row(lambda_k1[layer]), row(lambda_q2[layer]),
             row(lambda_k2[layer]), subln_g[layer].reshape(-1, 1)],
            seq, n_pairs=B_HEADS, q_off=0, k_off=B_HEADS, v_off=2 * B_HEADS, name="diff_attn")
        h = _mix_ffn(h, o_a, o_b, o_c, w_out, row(norm_ffn_g[layer]), w_up, conv_ffn_w[layer], w_down,
                     row(final_g), layer, seq, final_norm=(layer == depth - 1))
    return h.reshape(batch, seq, d)
```

```python
import functools
import math

import numpy as np
import jax
import jax.numpy as jnp
from jax import lax
from jax.experimental import pallas as pl
from jax.experimental.pallas import tpu as pltpu

D_MODEL = 1024
HEAD_DIM = 64
A_HEADS = 4
A_WIDTH = 256
B_HEADS = 4
B_QK = 512
B_V = 512
C_WIDTH = 256
IN_COLS = 3072
DILATED_PATTERNS = ((128, 1), (512, 4), (2048, 16))
CONV_WIDTH = 3
D_FF = 2816
ROPE_THETA = 500000.0
ROPE_DIMS = HEAD_DIM // 4
NORM_EPS = 1e-6
SUBLN_EPS = 1e-5

LANES = 128
CONV_HALO_ROWS = 8
MASK_BIAS = -1e30
LOG2_E = math.log2(math.e)
VMEM_LIMIT_BYTES = 56 * 1024 * 1024

F32 = jnp.float32
BF16 = jnp.bfloat16


def _rms_norm(x, g, eps):
    y = x * lax.rsqrt(jnp.mean(x * x, axis=-1, keepdims=True) + eps)
    return y * g


def _causal_conv3(p, prev, w):
    rows = lax.broadcasted_iota(jnp.int32, p.shape, 0)
    last = prev[CONV_HALO_ROWS - 1:CONV_HALO_ROWS, :]
    last2 = prev[CONV_HALO_ROWS - 2:CONV_HALO_ROWS - 1, :]
    p1 = jnp.where(rows == 0, last, pltpu.roll(p, 1, axis=0))
    p2 = jnp.where(rows == 0, last2, jnp.where(rows == 1, last, pltpu.roll(p, 2, axis=0)))
    return w[0:1, :] * p2 + w[1:2, :] * p1 + w[2:3, :] * p


ROPE_TOKENS_PER_ROW = LANES // ROPE_DIMS


def _rope_table_kernel(pos_ref, inv_ref, o_ref):
    ang = pos_ref[...].astype(F32) * inv_ref[...]
    lane = lax.broadcasted_iota(jnp.int32, ang.shape, 1) % ROPE_DIMS
    o_ref[...] = jnp.where(lane < ROPE_DIMS // 2, jnp.cos(ang), jnp.sin(ang))


def _rope_tables(pos):
    t = pos.shape[0]
    half = ROPE_DIMS // 2
    inv = ROPE_THETA ** (-(jnp.arange(half, dtype=F32) * 2.0 / ROPE_DIMS))
    inv_lane = jnp.tile(inv, LANES // half)[None, :]
    rows = t // ROPE_TOKENS_PER_ROW
    pos_rep = jnp.repeat(pos.reshape(rows, ROPE_TOKENS_PER_ROW), ROPE_DIMS, axis=1)
    compact = pl.pallas_call(
        _rope_table_kernel,
        grid=(1,),
        in_specs=[pl.BlockSpec((rows, LANES), lambda i: (0, 0)),
                  pl.BlockSpec((1, LANES), lambda i: (0, 0))],
        out_specs=pl.BlockSpec((rows, LANES), lambda i: (0, 0)),
        out_shape=jax.ShapeDtypeStruct((rows, LANES), F32),
        name="rope_tables",
    )(pos_rep, inv_lane)
    return compact


IN_ROWS = 512


def _rope_lane_tables(cs):
    half = ROPE_DIMS // 2
    groups = cs.shape[0]
    shape = (groups * ROPE_TOKENS_PER_ROW, LANES)
    spread = jnp.broadcast_to(cs[:, None, :], (groups, ROPE_TOKENS_PER_ROW, LANES)).reshape(shape)
    token = lax.broadcasted_iota(jnp.int32, shape, 0) % ROPE_TOKENS_PER_ROW
    lane = lax.broadcasted_iota(jnp.int32, shape, 1) % HEAD_DIM
    cos_idx = token * ROPE_DIMS + lane % half
    cos = jnp.take_along_axis(spread, cos_idx, axis=1)
    sin = jnp.take_along_axis(spread, cos_idx + half, axis=1)
    return (jnp.where(lane < ROPE_DIMS, cos, 1.0),
            jnp.where((lane >= half) & (lane < ROPE_DIMS), sin, 0.0),
            jnp.where(lane < half, -sin, 0.0))


def _in_proj_kernel(h_ref, g_ref, w_hbm, cs_ref, cw_ref,
                    a_ref, b_ref, oc_ref, hn_ref, carry_ref, w_ref, w_stage, w_sem,
                    *, layer, tiles_per_seq):
    i = pl.program_id(0)

    @pl.when(i == 0)
    def _():
        _load_weights_bf16([(w_hbm, w_ref)], layer, w_stage, w_sem)

    @pl.when(i % tiles_per_seq == 0)
    def _():
        carry_ref[...] = jnp.zeros_like(carry_ref)

    hn_ref[...] = _rms_norm(h_ref[...], g_ref[...], NORM_EPS).astype(BF16)

    cos, sin_up, sin_dn = _rope_lane_tables(cs_ref[...])
    half = ROPE_DIMS // 2

    def proj(c0, width):
        return jnp.dot(hn_ref[...], w_ref[:, c0:c0 + width], preferred_element_type=F32)

    def rope(y):
        return (y * cos + pltpu.roll(y, half, axis=1) * sin_up
                + pltpu.roll(y, LANES - half, axis=1) * sin_dn)

    def rope_cols(c0, width, scale, out_ref, o0):
        y = proj(c0, width)
        for k in range(width // LANES):
            r = rope(y[:, k * LANES:(k + 1) * LANES])
            if scale != 1.0:
                r = r * scale
            out_ref[:, o0 + k * LANES:o0 + (k + 1) * LANES] = r.astype(BF16)

    scale = HEAD_DIM ** -0.5 * LOG2_E
    b0 = 3 * A_WIDTH
    c0 = b0 + 2 * B_QK + B_V
    gate = proj(c0, C_WIDTH)
    prod = proj(c0 + C_WIDTH, C_WIDTH) * proj(c0 + 2 * C_WIDTH, C_WIDTH)
    prev = carry_ref[...]
    carry_ref[...] = prod[prod.shape[0] - CONV_HALO_ROWS:, :]
    oc_ref[...] = (gate * _causal_conv3(prod, prev, cw_ref[...])).astype(BF16)

    rope_cols(0, A_WIDTH, scale, a_ref, 0)
    rope_cols(A_WIDTH, A_WIDTH, 1.0, a_ref, A_WIDTH)
    rope_cols(b0, B_QK, scale, b_ref, 0)
    rope_cols(b0 + B_QK, B_QK, 1.0, b_ref, B_QK)
    a_ref[:, 2 * A_WIDTH:3 * A_WIDTH] = proj(2 * A_WIDTH, A_WIDTH).astype(BF16)
    b_ref[:, 2 * B_QK:2 * B_QK + B_V] = proj(b0 + 2 * B_QK, B_V).astype(BF16)


WEIGHT_CHUNK = (256, 512)
WEIGHT_RING = 6


def _weight_ring_scratch():
    return [pltpu.VMEM((WEIGHT_RING,) + WEIGHT_CHUNK, F32), pltpu.SemaphoreType.DMA((WEIGHT_RING,))]


def _load_weights_bf16(jobs, layer, stage_ref, sem):
    cr, cc = WEIGHT_CHUNK
    chunks = [(w, dst, r0, c0) for w, dst in jobs
              for r0 in range(0, dst.shape[0], cr) for c0 in range(0, dst.shape[1], cc)]

    def copy(k):
        w, _, r0, c0 = chunks[k]
        slot = k % WEIGHT_RING
        return pltpu.make_async_copy(w.at[layer, r0:r0 + cr, c0:c0 + cc], stage_ref.at[slot],
                                     sem.at[slot])

    for k in range(min(WEIGHT_RING, len(chunks))):
        copy(k).start()
    for k, (_, dst, r0, c0) in enumerate(chunks):
        copy(k).wait()
        dst[r0:r0 + cr, c0:c0 + cc] = stage_ref[k % WEIGHT_RING].astype(BF16)
        if k + WEIGHT_RING < len(chunks):
            copy(k + WEIGHT_RING).start()


def _in_proj(h, g, w, layer, rope_cs, conv_w, seq):
    t = h.shape[0]
    rows = IN_ROWS
    row_spec = lambda width: pl.BlockSpec((rows, width), lambda i: (i, 0))
    full = lambda shape: pl.BlockSpec(shape, lambda i: (0, 0))
    return pl.pallas_call(
        functools.partial(_in_proj_kernel, layer=layer, tiles_per_seq=seq // rows),
        grid=(t // rows,),
        in_specs=[row_spec(D_MODEL), full((1, D_MODEL)), pl.BlockSpec(memory_space=pl.ANY),
                  pl.BlockSpec((rows // ROPE_TOKENS_PER_ROW, LANES), lambda i: (i, 0)),
                  full((CONV_WIDTH, C_WIDTH))],
        out_specs=[row_spec(3 * A_WIDTH), row_spec(2 * B_QK + B_V), row_spec(C_WIDTH)],
        out_shape=[jax.ShapeDtypeStruct((t, 3 * A_WIDTH), BF16),
                   jax.ShapeDtypeStruct((t, 2 * B_QK + B_V), BF16),
                   jax.ShapeDtypeStruct((t, C_WIDTH), BF16)],
        scratch_shapes=[pltpu.VMEM((rows, D_MODEL), BF16),
                        pltpu.VMEM((CONV_HALO_ROWS, C_WIDTH), F32),
                        pltpu.VMEM(w.shape[1:], BF16)] + _weight_ring_scratch(),
        compiler_params=pltpu.CompilerParams(dimension_semantics=("arbitrary",),
                                             vmem_limit_bytes=VMEM_LIMIT_BYTES),
        name="in_proj",
    )(h, g, w, rope_cs, conv_w)


ATT_ROWS = 256
SCORE_SLOTS = 2
ONES_ROWS = 16


def _dilated_bias_table(seq):
    u = np.arange(seq)[:, None]
    r = np.arange(ATT_ROWS)[None, :]
    delta = r + seq - ATT_ROWS - u
    count = np.zeros(delta.shape, np.int64)
    for window, dil in DILATED_PATTERNS:
        count += (delta >= 0) & (delta <= window) & (delta % dil == 0)
    return np.where(count > 0, np.log2(np.maximum(count, 1)), MASK_BIAS).astype(np.float32)


def _causal_bias_table():
    key = np.arange(ATT_ROWS)[:, None]
    query = np.arange(ATT_ROWS)[None, :]
    return np.where(key <= query, 0.0, MASK_BIAS).astype(np.float32)


def _pair_attention(q_ref, k_ref, v_ref, vt_ref, s_refs, block_bias, emit):
    rows = ATT_ROWS
    n_chunks = q_ref.shape[0] // rows
    vt_ref[0:LANES, :] = v_ref[...].astype(F32).T.astype(BF16)
    vt_ref[LANES:, :] = jnp.ones((ONES_ROWS, vt_ref.shape[1]), BF16)
    lane = lax.broadcasted_iota(jnp.int32, (rows, LANES), 1)

    def stacked_queries(c):
        q = q_ref[c * rows:(c + 1) * rows, :]
        zero = jnp.zeros_like(q)
        return jnp.concatenate([jnp.where(lane < HEAD_DIM, q, zero),
                                jnp.where(lane >= HEAD_DIM, q, zero)], axis=0)

    half = rows // 2
    nt_dims = (((1,), (1,)), ((), ()))

    def late_lanes(x):
        return jnp.concatenate([x[:, half:rows], x[:, rows + half:]], axis=1)

    def late_lanes_only(x, fill):
        pad = jnp.full((x.shape[0], half), fill, x.dtype)
        return jnp.concatenate([pad, x[:, :half], pad, x[:, half:]], axis=1)

    def sublane_max(s):
        return jnp.max(s.reshape(s.shape[0] // 8, 8, s.shape[1]), axis=0)

    def score_block(c, j, qs, col_max):
        slot = s_refs[c % SCORE_SLOTS]
        bias = block_bias(c, j)
        if j < c:
            s = lax.dot_general(k_ref[j * rows:(j + 1) * rows, :], qs, nt_dims,
                                preferred_element_type=F32)
            if bias is not None:
                s = s + jnp.concatenate([bias, bias], axis=1)
            slot[j * rows:(j + 1) * rows, :] = s
            blk_max = sublane_max(s)
        else:
            mid = j * rows + half
            s = lax.dot_general(k_ref[j * rows:mid, :], qs, nt_dims, preferred_element_type=F32)
            s = s + jnp.concatenate([bias[:half], bias[:half]], axis=1)
            slot[j * rows:mid, :] = s
            qs_late = jnp.concatenate([qs[half:rows], qs[rows + half:]], axis=0)
            s_late = lax.dot_general(k_ref[mid:mid + half, :], qs_late, nt_dims,
                                     preferred_element_type=F32)
            s_late = s_late + jnp.concatenate([bias[half:, half:], bias[half:, half:]], axis=1)
            slot[mid:mid + half, 0:rows] = s_late
            blk_max = jnp.maximum(sublane_max(s), late_lanes_only(sublane_max(s_late), MASK_BIAS))
        return blk_max if col_max is None else jnp.maximum(col_max, blk_max)

    def value_block(c, j, m, acc):
        slot = s_refs[c % SCORE_SLOTS]
        if j < c:
            p = jnp.exp2(slot[j * rows:(j + 1) * rows, :] - m).astype(BF16)
            part = jnp.dot(vt_ref[:, j * rows:(j + 1) * rows], p, preferred_element_type=F32)
            return part if acc is None else acc + part
        mid = j * rows + half
        p = jnp.exp2(slot[j * rows:mid, :] - m).astype(BF16)
        part = jnp.dot(vt_ref[:, j * rows:mid], p, preferred_element_type=F32)
        acc = part if acc is None else acc + part
        p_late = jnp.exp2(slot[mid:mid + half, 0:rows] - late_lanes(m)).astype(BF16)
        part = jnp.dot(vt_ref[:, mid:mid + half], p_late, preferred_element_type=F32)
        return acc + late_lanes_only(part, 0.0)

    qs = stacked_queries(0)
    col_max = score_block(0, 0, qs, None)
    acc = None
    for c in range(n_chunks):
        m = jnp.max(col_max, axis=0, keepdims=True)
        nxt = c + 1
        if nxt < n_chunks:
            qs = stacked_queries(nxt)
        acc = None
        col_max = None
        for j in range(nxt + 1):
            if nxt < n_chunks:
                col_max = score_block(nxt, j, qs, col_max)
            if j <= c:
                acc = value_block(c, j, m, acc)
        emit(c, acc[0:LANES], acc[LANES:LANES + 1])


def _dilated_attn_kernel(q_ref, k_ref, v_ref, bias_ref, o_ref, vt_ref, *s_refs):
    rows = ATT_ROWS
    seq = q_ref.shape[0]

    def block_bias(c, j):
        first = seq - (c + 1 - j) * rows
        return bias_ref[first:first + rows, :]

    def emit(c, o_t, l):
        o_t = o_t * (1.0 / l)
        o = jnp.concatenate([o_t[:HEAD_DIM, :rows], o_t[HEAD_DIM:, rows:]], axis=0)
        o_ref[c * rows:(c + 1) * rows, :] = o.T.astype(o_ref.dtype)

    _pair_attention(q_ref, k_ref, v_ref, vt_ref, s_refs, block_bias, emit)


def _diff_attn_kernel(q_ref, k_ref, v_ref, bias_ref, lq1_ref, lk1_ref, lq2_ref, lk2_ref, g_ref,
                      o_ref, vt_ref, *s_refs, lam_init):
    rows = ATT_ROWS
    lam = (jnp.exp(jnp.sum(lq1_ref[...] * lk1_ref[...], axis=1, keepdims=True))
           - jnp.exp(jnp.sum(lq2_ref[...] * lk2_ref[...], axis=1, keepdims=True)) + lam_init)
    gain = g_ref[...]

    def block_bias(c, j):
        return bias_ref[...] if j == c else None

    def emit(c, o_t, l):
        o_t = o_t * (1.0 / l)
        o = o_t[:, :rows] - lam * o_t[:, rows:]
        o = o * lax.rsqrt(jnp.mean(o * o, axis=0, keepdims=True) + SUBLN_EPS) * gain
        o_ref[c * rows:(c + 1) * rows, :] = (o * (1.0 - lam_init)).T.astype(o_ref.dtype)

    _pair_attention(q_ref, k_ref, v_ref, vt_ref, s_refs, block_bias, emit)


def _attention_call(body, qkv, bias, extra, seq, n_pairs, q_off, k_off, v_off, name):
    t = qkv.shape[0]
    batch = t // seq
    small = lambda a: pl.BlockSpec(a.shape, lambda b, p: (0,) * a.ndim)
    col = lambda off: pl.BlockSpec((seq, LANES), lambda b, p: (b, off + p))
    return pl.pallas_call(
        body,
        grid=(batch, n_pairs),
        in_specs=[col(q_off), col(k_off), col(v_off), small(bias)] + [small(a) for a in extra],
        out_specs=col(0),
        out_shape=jax.ShapeDtypeStruct((t, n_pairs * LANES), BF16),
        scratch_shapes=([pltpu.VMEM((LANES + ONES_ROWS, seq), BF16)]
                        + [pltpu.VMEM((seq, 2 * ATT_ROWS), F32)] * SCORE_SLOTS),
        compiler_params=pltpu.CompilerParams(dimension_semantics=("arbitrary", "arbitrary"),
                                             vmem_limit_bytes=VMEM_LIMIT_BYTES),
        name=name,
    )(qkv, qkv, qkv, bias, *extra)


FFN_ROWS = 512
FFN_CHUNK = 512


def _mix_ffn_kernel(h_ref, oa_ref, ob_ref, oc_ref, wo_hbm, g_ref, wup_hbm, cw_ref, wd_hbm, fg_ref,
                    o_ref, hn_ref, act_ref, carry_ref, wo_ref, wup_ref, wd_ref, w_stage, w_sem,
                    *, layer, tiles_per_seq, final_norm):
    i = pl.program_id(0)

    @pl.when(i == 0)
    def _():
        _load_weights_bf16([(wo_hbm, wo_ref), (wup_hbm, wup_ref), (wd_hbm, wd_ref)], layer,
                           w_stage, w_sem)

    @pl.when(i % tiles_per_seq == 0)
    def _():
        carry_ref[...] = jnp.zeros_like(carry_ref)

    mix = jnp.dot(oa_ref[...], wo_ref[0:A_WIDTH, :], preferred_element_type=F32)
    mix += jnp.dot(ob_ref[...], wo_ref[A_WIDTH:A_WIDTH + B_V, :], preferred_element_type=F32)
    mix += jnp.dot(oc_ref[...], wo_ref[A_WIDTH + B_V:, :], preferred_element_type=F32)
    h_mid = h_ref[...] + mix
    o_ref[...] = h_mid
    hn_ref[...] = _rms_norm(h_mid, g_ref[...], NORM_EPS).astype(BF16)

    def gate_up(cols):
        up_cols = slice(D_FF + cols.start, D_FF + cols.stop)
        hn = hn_ref[...]
        return (jnp.dot(hn, wup_ref[:, cols], preferred_element_type=F32),
                jnp.dot(hn, wup_ref[:, up_cols], preferred_element_type=F32))

    def activate(cols, gate, up):
        prev = carry_ref[:, cols]
        carry_ref[:, cols] = gate[gate.shape[0] - CONV_HALO_ROWS:, :]
        gc = _causal_conv3(gate, prev, cw_ref[:, cols])
        act_ref[:, cols] = (gc / (1.0 + jnp.exp(-gc)) * up).astype(BF16)

    chunks = [slice(c0, min(c0 + FFN_CHUNK, D_FF)) for c0 in range(0, D_FF, FFN_CHUNK)]
    pending = gate_up(chunks[0])
    for c, cols in enumerate(chunks):
        nxt = gate_up(chunks[c + 1]) if c + 1 < len(chunks) else None
        activate(cols, *pending)
        pending = nxt

    out = o_ref[...] + jnp.dot(act_ref[...], wd_ref[...], preferred_element_type=F32)
    if final_norm:
        out = _rms_norm(out, fg_ref[...], NORM_EPS)
    o_ref[...] = out


def _mix_ffn(h, o_a, o_b, o_c, w_out, g, w_up, conv_w, w_down, final_g, layer, seq, final_norm):
    t = h.shape[0]
    rows = FFN_ROWS
    row_spec = lambda width: pl.BlockSpec((rows, width), lambda i: (i, 0))
    full = lambda shape: pl.BlockSpec(shape, lambda i: (0, 0))
    hbm = pl.BlockSpec(memory_space=pl.ANY)
    return pl.pallas_call(
        functools.partial(_mix_ffn_kernel, layer=layer, tiles_per_seq=seq // rows,
                          final_norm=final_norm),
        grid=(t // rows,),
        in_specs=[row_spec(D_MODEL), row_spec(A_WIDTH), row_spec(B_V), row_spec(C_WIDTH),
                  hbm, full((1, D_MODEL)), hbm, full((CONV_WIDTH, D_FF)), hbm, full((1, D_MODEL))],
        out_specs=row_spec(D_MODEL),
        out_shape=jax.ShapeDtypeStruct((t, D_MODEL), F32),
        scratch_shapes=([pltpu.VMEM((rows, D_MODEL), BF16),
                         pltpu.VMEM((rows, D_FF), BF16),
                         pltpu.VMEM((CONV_HALO_ROWS, D_FF), F32)]
                        + [pltpu.VMEM(w.shape[1:], BF16) for w in (w_out, w_up, w_down)]
                        + _weight_ring_scratch()),
        compiler_params=pltpu.CompilerParams(dimension_semantics=("arbitrary",),
                                             vmem_limit_bytes=VMEM_LIMIT_BYTES),
        name="mix_ffn",
    )(h, o_a, o_b, o_c, w_out, g, w_up, conv_w, w_down, final_g)


def kernel(x, positions, norm_mix_g, w_in, lambda_q1, lambda_k1, lambda_q2, lambda_k2, subln_g,
           conv_mix_w, w_out, norm_ffn_g, w_up, conv_ffn_w, w_down, final_g):
    batch, seq, d = x.shape
    depth = w_in.shape[0]
    assert d == D_MODEL and seq % ATT_ROWS == 0 and seq % IN_ROWS == 0 and seq % FFN_ROWS == 0
    t = batch * seq
    h = x.reshape(t, d)
    tables = _rope_tables(positions.reshape(t))
    bias_a = jnp.asarray(_dilated_bias_table(seq))
    bias_b = jnp.asarray(_causal_bias_table())
    row = lambda v: v.reshape(1, -1)

    for layer in range(depth):
        a_qkv, b_qkv, o_c = _in_proj(h, row(norm_mix_g[layer]), w_in, layer, tables,
                                     conv_mix_w[layer], seq)
        o_a = _attention_call(_dilated_attn_kernel, a_qkv, bias_a, [], seq,
                              n_pairs=A_HEADS // 2, q_off=0, k_off=2, v_off=4, name="dilated_attn")
        lam_init = 0.8 - 0.6 * math.exp(-0.3 * layer)
        o_b = _attention_call(
            functools.partial(_diff_attn_kernel, lam_init=lam_init), b_qkv, bias_b,
            [row(lambda_q1[layer]), row(lambda_k1[layer]), row(lambda_q2[layer]),
             row(lambda_k2[layer]), subln_g[layer].reshape(-1, 1)],
            seq, n_pairs=B_HEADS, q_off=0, k_off=B_HEADS, v_off=2 * B_HEADS, name="diff_attn")
        h = _mix_ffn(h, o_a, o_b, o_c, w_out, row(norm_ffn_g[layer]), w_up, conv_ffn_w[layer], w_down,
                     row(final_g), layer, seq, final_norm=(layer == depth - 1))
    return h.reshape(batch, seq, d)
```

```python
import functools
import math

import numpy as np
import jax
import jax.numpy as jnp
from jax import lax
from jax.experimental import pallas as pl
from jax.experimental.pallas import tpu as pltpu

D_MODEL = 1024
HEAD_DIM = 64
A_HEADS = 4
A_WIDTH = 256
B_HEADS = 4
B_QK = 512
B_V = 512
C_WIDTH = 256
IN_COLS = 3072
DILATED_PATTERNS = ((128, 1), (512, 4), (2048, 16))
CONV_WIDTH = 3
D_FF = 2816
ROPE_THETA = 500000.0
ROPE_DIMS = HEAD_DIM // 4
NORM_EPS = 1e-6
SUBLN_EPS = 1e-5

LANES = 128
CONV_HALO_ROWS = 8
MASK_BIAS = -1e30
LOG2_E = math.log2(math.e)
VMEM_LIMIT_BYTES = 56 * 1024 * 1024

F32 = jnp.float32
BF16 = jnp.bfloat16


def _rms_norm(x, g, eps):
    y = x * lax.rsqrt(jnp.mean(x * x, axis=-1, keepdims=True) + eps)
    return y * g


def _causal_conv3(p, prev, w):
    rows = lax.broadcasted_iota(jnp.int32, p.shape, 0)
    last = prev[CONV_HALO_ROWS - 1:CONV_HALO_ROWS, :]
    last2 = prev[CONV_HALO_ROWS - 2:CONV_HALO_ROWS - 1, :]
    p1 = jnp.where(rows == 0, last, pltpu.roll(p, 1, axis=0))
    p2 = jnp.where(rows == 0, last2, jnp.where(rows == 1, last, pltpu.roll(p, 2, axis=0)))
    return w[0:1, :] * p2 + w[1:2, :] * p1 + w[2:3, :] * p


ROPE_TOKENS_PER_ROW = LANES // ROPE_DIMS


def _rope_table_kernel(pos_ref, inv_ref, o_ref):
    ang = pos_ref[...].astype(F32) * inv_ref[...]
    lane = lax.broadcasted_iota(jnp.int32, ang.shape, 1) % ROPE_DIMS
    o_ref[...] = jnp.where(lane < ROPE_DIMS // 2, jnp.cos(ang), jnp.sin(ang))


def _rope_tables(pos):
    t = pos.shape[0]
    half = ROPE_DIMS // 2
    inv = ROPE_THETA ** (-(jnp.arange(half, dtype=F32) * 2.0 / ROPE_DIMS))
    inv_lane = jnp.tile(inv, LANES // half)[None, :]
    rows = t // ROPE_TOKENS_PER_ROW
    pos_rep = jnp.repeat(pos.reshape(rows, ROPE_TOKENS_PER_ROW), ROPE_DIMS, axis=1)
    compact = pl.pallas_call(
        _rope_table_kernel,
        grid=(1,),
        in_specs=[pl.BlockSpec((rows, LANES), lambda i: (0, 0)),
                  pl.BlockSpec((1, LANES), lambda i: (0, 0))],
        out_specs=pl.BlockSpec((rows, LANES), lambda i: (0, 0)),
        out_shape=jax.ShapeDtypeStruct((rows, LANES), F32),
        name="rope_tables",
    )(pos_rep, inv_lane)
    return compact


IN_ROWS = 512


def _rope_lane_tables(cs):
    half = ROPE_DIMS // 2
    groups = cs.shape[0]
    shape = (groups * ROPE_TOKENS_PER_ROW, LANES)
    spread = jnp.broadcast_to(cs[:, None, :], (groups, ROPE_TOKENS_PER_ROW, LANES)).reshape(shape)
    token = lax.broadcasted_iota(jnp.int32, shape, 0) % ROPE_TOKENS_PER_ROW
    lane = lax.broadcasted_iota(jnp.int32, shape, 1) % HEAD_DIM
    cos_idx = token * ROPE_DIMS + lane % half
    cos = jnp.take_along_axis(spread, cos_idx, axis=1)
    sin = jnp.take_along_axis(spread, cos_idx + half, axis=1)
    return (jnp.where(lane < ROPE_DIMS, cos, 1.0),
            jnp.where((lane >= half) & (lane < ROPE_DIMS), sin, 0.0),
            jnp.where(lane < half, -sin, 0.0))


def _in_proj_kernel(h_ref, g_ref, w_hbm, cs_ref, cw_ref,
                    a_ref, b_ref, oc_ref, hn_ref, carry_ref, w_ref, w_stage, w_sem,
                    *, layer, tiles_per_seq):
    i = pl.program_id(0)

    @pl.when(i == 0)
    def _():
        _load_weights_bf16([(w_hbm, w_ref)], layer, w_stage, w_sem)

    hn_ref[...] = _rms_norm(h_ref[...], g_ref[...], NORM_EPS).astype(BF16)

    cos, sin_up, sin_dn = _rope_lane_tables(cs_ref[...])
    half = ROPE_DIMS // 2

    def proj(c0, width):
        return jnp.dot(hn_ref[...], w_ref[:, c0:c0 + width], preferred_element_type=F32)

    def rope(y):
        return (y * cos + pltpu.roll(y, half, axis=1) * sin_up
                + pltpu.roll(y, LANES - half, axis=1) * sin_dn)

    def rope_cols(c0, width, scale, out_ref, o0):
        y = proj(c0, width)
        for k in range(width // LANES):
            r = rope(y[:, k * LANES:(k + 1) * LANES])
            if scale != 1.0:
                r = r * scale
            out_ref[:, o0 + k * LANES:o0 + (k + 1) * LANES] = r.astype(BF16)

    scale = HEAD_DIM ** -0.5 * LOG2_E
    rope_cols(0, A_WIDTH, scale, a_ref, 0)
    rope_cols(A_WIDTH, A_WIDTH, 1.0, a_ref, A_WIDTH)
    a_ref[:, 2 * A_WIDTH:3 * A_WIDTH] = proj(2 * A_WIDTH, A_WIDTH).astype(BF16)
    b0 = 3 * A_WIDTH
    rope_cols(b0, B_QK, scale, b_ref, 0)
    rope_cols(b0 + B_QK, B_QK, 1.0, b_ref, B_QK)
    b_ref[:, 2 * B_QK:2 * B_QK + B_V] = proj(b0 + 2 * B_QK, B_V).astype(BF16)

    c0 = b0 + 2 * B_QK + B_V
    gate = proj(c0, C_WIDTH)
    prod = proj(c0 + C_WIDTH, C_WIDTH) * proj(c0 + 2 * C_WIDTH, C_WIDTH)

    @pl.when(i % tiles_per_seq == 0)
    def _():
        carry_ref[...] = jnp.zeros_like(carry_ref)

    prev = carry_ref[...]
    carry_ref[...] = prod[prod.shape[0] - CONV_HALO_ROWS:, :]
    oc_ref[...] = (gate * _causal_conv3(prod, prev, cw_ref[...])).astype(BF16)


WEIGHT_CHUNK = (256, 512)
WEIGHT_RING = 6


def _weight_ring_scratch():
    return [pltpu.VMEM((WEIGHT_RING,) + WEIGHT_CHUNK, F32), pltpu.SemaphoreType.DMA((WEIGHT_RING,))]


def _load_weights_bf16(jobs, layer, stage_ref, sem):
    cr, cc = WEIGHT_CHUNK
    chunks = [(w, dst, r0, c0) for w, dst in jobs
              for r0 in range(0, dst.shape[0], cr) for c0 in range(0, dst.shape[1], cc)]

    def copy(k):
        w, _, r0, c0 = chunks[k]
        slot = k % WEIGHT_RING
        return pltpu.make_async_copy(w.at[layer, r0:r0 + cr, c0:c0 + cc], stage_ref.at[slot],
                                     sem.at[slot])

    for k in range(min(WEIGHT_RING, len(chunks))):
        copy(k).start()
    for k, (_, dst, r0, c0) in enumerate(chunks):
        copy(k).wait()
        dst[r0:r0 + cr, c0:c0 + cc] = stage_ref[k % WEIGHT_RING].astype(BF16)
        if k + WEIGHT_RING < len(chunks):
            copy(k + WEIGHT_RING).start()


def _in_proj(h, g, w, layer, rope_cs, conv_w, seq):
    t = h.shape[0]
    rows = IN_ROWS
    row_spec = lambda width: pl.BlockSpec((rows, width), lambda i: (i, 0))
    full = lambda shape: pl.BlockSpec(shape, lambda i: (0, 0))
    return pl.pallas_call(
        functools.partial(_in_proj_kernel, layer=layer, tiles_per_seq=seq // rows),
        grid=(t // rows,),
        in_specs=[row_spec(D_MODEL), full((1, D_MODEL)), pl.BlockSpec(memory_space=pl.ANY),
                  pl.BlockSpec((rows // ROPE_TOKENS_PER_ROW, LANES), lambda i: (i, 0)),
                  full((CONV_WIDTH, C_WIDTH))],
        out_specs=[row_spec(3 * A_WIDTH), row_spec(2 * B_QK + B_V), row_spec(C_WIDTH)],
        out_shape=[jax.ShapeDtypeStruct((t, 3 * A_WIDTH), BF16),
                   jax.ShapeDtypeStruct((t, 2 * B_QK + B_V), BF16),
                   jax.ShapeDtypeStruct((t, C_WIDTH), BF16)],
        scratch_shapes=[pltpu.VMEM((rows, D_MODEL), BF16),
                        pltpu.VMEM((CONV_HALO_ROWS, C_WIDTH), F32),
                        pltpu.VMEM(w.shape[1:], BF16)] + _weight_ring_scratch(),
        compiler_params=pltpu.CompilerParams(dimension_semantics=("arbitrary",),
                                             vmem_limit_bytes=VMEM_LIMIT_BYTES),
        name="in_proj",
    )(h, g, w, rope_cs, conv_w)


ATT_ROWS = 512
SCORE_SLOTS = 2
ONES_ROWS = 16


def _dilated_bias_table(seq):
    u = np.arange(seq)[:, None]
    r = np.arange(ATT_ROWS)[None, :]
    delta = r + seq - ATT_ROWS - u
    count = np.zeros(delta.shape, np.int64)
    for window, dil in DILATED_PATTERNS:
        count += (delta >= 0) & (delta <= window) & (delta % dil == 0)
    return np.where(count > 0, np.log2(np.maximum(count, 1)), MASK_BIAS).astype(np.float32)


def _causal_bias_table():
    key = np.arange(ATT_ROWS)[:, None]
    query = np.arange(ATT_ROWS)[None, :]
    return np.where(key <= query, 0.0, MASK_BIAS).astype(np.float32)


def _pair_attention(q_ref, k_ref, v_ref, vt_ref, s_refs, block_bias, emit):
    rows = ATT_ROWS
    n_chunks = q_ref.shape[0] // rows
    vt_ref[0:LANES, :] = v_ref[...].astype(F32).T.astype(BF16)
    vt_ref[LANES:, :] = jnp.ones((ONES_ROWS, vt_ref.shape[1]), BF16)
    lane = lax.broadcasted_iota(jnp.int32, (rows, LANES), 1)

    def stacked_queries(c):
        q = q_ref[c * rows:(c + 1) * rows, :]
        zero = jnp.zeros_like(q)
        return jnp.concatenate([jnp.where(lane < HEAD_DIM, q, zero),
                                jnp.where(lane >= HEAD_DIM, q, zero)], axis=0)

    half = rows // 2
    nt_dims = (((1,), (1,)), ((), ()))

    def late_lanes(x):
        return jnp.concatenate([x[:, half:rows], x[:, rows + half:]], axis=1)

    def late_lanes_only(x, fill):
        pad = jnp.full((x.shape[0], half), fill, x.dtype)
        return jnp.concatenate([pad, x[:, :half], pad, x[:, half:]], axis=1)

    def sublane_max(s):
        return jnp.max(s.reshape(s.shape[0] // 8, 8, s.shape[1]), axis=0)

    def score_block(c, j, qs, col_max):
        slot = s_refs[c % SCORE_SLOTS]
        bias = block_bias(c, j)
        if j < c:
            s = lax.dot_general(k_ref[j * rows:(j + 1) * rows, :], qs, nt_dims,
                                preferred_element_type=F32)
            if bias is not None:
                s = s + jnp.concatenate([bias, bias], axis=1)
            slot[j * rows:(j + 1) * rows, :] = s
            blk_max = sublane_max(s)
        else:
            mid = j * rows + half
            s = lax.dot_general(k_ref[j * rows:mid, :], qs, nt_dims, preferred_element_type=F32)
            s = s + jnp.concatenate([bias[:half], bias[:half]], axis=1)
            slot[j * rows:mid, :] = s
            qs_late = jnp.concatenate([qs[half:rows], qs[rows + half:]], axis=0)
            s_late = lax.dot_general(k_ref[mid:mid + half, :], qs_late, nt_dims,
                                     preferred_element_type=F32)
            s_late = s_late + jnp.concatenate([bias[half:, half:], bias[half:, half:]], axis=1)
            slot[mid:mid + half, 0:rows] = s_late
            blk_max = jnp.maximum(sublane_max(s), late_lanes_only(sublane_max(s_late), MASK_BIAS))
        return blk_max if col_max is None else jnp.maximum(col_max, blk_max)

    def value_block(c, j, m, acc):
        slot = s_refs[c % SCORE_SLOTS]
        if j < c:
            p = jnp.exp2(slot[j * rows:(j + 1) * rows, :] - m).astype(BF16)
            part = jnp.dot(vt_ref[:, j * rows:(j + 1) * rows], p, preferred_element_type=F32)
            return part if acc is None else acc + part
        mid = j * rows + half
        p = jnp.exp2(slot[j * rows:mid, :] - m).astype(BF16)
        part = jnp.dot(vt_ref[:, j * rows:mid], p, preferred_element_type=F32)
        acc = part if acc is None else acc + part
        p_late = jnp.exp2(slot[mid:mid + half, 0:rows] - late_lanes(m)).astype(BF16)
        part = jnp.dot(vt_ref[:, mid:mid + half], p_late, preferred_element_type=F32)
        return acc + late_lanes_only(part, 0.0)

    qs = stacked_queries(0)
    col_max = score_block(0, 0, qs, None)
    acc = None
    for c in range(n_chunks):
        m = jnp.max(col_max, axis=0, keepdims=True)
        nxt = c + 1
        if nxt < n_chunks:
            qs = stacked_queries(nxt)
        acc = None
        col_max = None
        for j in range(nxt + 1):
            if nxt < n_chunks:
                col_max = score_block(nxt, j, qs, col_max)
            if j <= c:
                acc = value_block(c, j, m, acc)
        emit(c, acc[0:LANES], acc[LANES:LANES + 1])


def _dilated_attn_kernel(q_ref, k_ref, v_ref, bias_ref, o_ref, vt_ref, *s_refs):
    rows = ATT_ROWS
    seq = q_ref.shape[0]

    def block_bias(c, j):
        first = seq - (c + 1 - j) * rows
        return bias_ref[first:first + rows, :]

    def emit(c, o_t, l):
        o_t = o_t * (1.0 / l)
        o = jnp.concatenate([o_t[:HEAD_DIM, :rows], o_t[HEAD_DIM:, rows:]], axis=0)
        o_ref[c * rows:(c + 1) * rows, :] = o.T.astype(o_ref.dtype)

    _pair_attention(q_ref, k_ref, v_ref, vt_ref, s_refs, block_bias, emit)


def _diff_attn_kernel(q_ref, k_ref, v_ref, bias_ref, lq1_ref, lk1_ref, lq2_ref, lk2_ref, g_ref,
                      o_ref, vt_ref, *s_refs, lam_init):
    rows = ATT_ROWS
    lam = (jnp.exp(jnp.sum(lq1_ref[...] * lk1_ref[...], axis=1, keepdims=True))
           - jnp.exp(jnp.sum(lq2_ref[...] * lk2_ref[...], axis=1, keepdims=True)) + lam_init)
    gain = g_ref[...]

    def block_bias(c, j):
        return bias_ref[...] if j == c else None

    def emit(c, o_t, l):
        o_t = o_t * (1.0 / l)
        o = o_t[:, :rows] - lam * o_t[:, rows:]
        o = o * lax.rsqrt(jnp.mean(o * o, axis=0, keepdims=True) + SUBLN_EPS) * gain
        o_ref[c * rows:(c + 1) * rows, :] = (o * (1.0 - lam_init)).T.astype(o_ref.dtype)

    _pair_attention(q_ref, k_ref, v_ref, vt_ref, s_refs, block_bias, emit)


def _attention_call(body, qkv, bias, extra, seq, n_pairs, q_off, k_off, v_off, name):
    t = qkv.shape[0]
    batch = t // seq
    small = lambda a: pl.BlockSpec(a.shape, lambda b, p: (0,) * a.ndim)
    col = lambda off: pl.BlockSpec((seq, LANES), lambda b, p: (b, off + p))
    return pl.pallas_call(
        body,
        grid=(batch, n_pairs),
        in_specs=[col(q_off), col(k_off), col(v_off), small(bias)] + [small(a) for a in extra],
        out_specs=col(0),
        out_shape=jax.ShapeDtypeStruct((t, n_pairs * LANES), BF16),
        scratch_shapes=([pltpu.VMEM((LANES + ONES_ROWS, seq), BF16)]
                        + [pltpu.VMEM((seq, 2 * ATT_ROWS), F32)] * SCORE_SLOTS),
        compiler_params=pltpu.CompilerParams(dimension_semantics=("arbitrary", "arbitrary"),
                                             vmem_limit_bytes=VMEM_LIMIT_BYTES),
        name=name,
    )(qkv, qkv, qkv, bias, *extra)


FFN_ROWS = 512
FFN_CHUNK = 512


def _mix_ffn_kernel(h_ref, oa_ref, ob_ref, oc_ref, wo_hbm, g_ref, wup_hbm, cw_ref, wd_hbm, fg_ref,
                    o_ref, hn_ref, act_ref, carry_ref, wo_ref, wup_ref, wd_ref, w_stage, w_sem,
                    *, layer, tiles_per_seq, final_norm):
    i = pl.program_id(0)

    @pl.when(i == 0)
    def _():
        _load_weights_bf16([(wo_hbm, wo_ref), (wup_hbm, wup_ref), (wd_hbm, wd_ref)], layer,
                           w_stage, w_sem)

    @pl.when(i % tiles_per_seq == 0)
    def _():
        carry_ref[...] = jnp.zeros_like(carry_ref)

    mix = jnp.dot(oa_ref[...], wo_ref[0:A_WIDTH, :], preferred_element_type=F32)
    mix += jnp.dot(ob_ref[...], wo_ref[A_WIDTH:A_WIDTH + B_V, :], preferred_element_type=F32)
    mix += jnp.dot(oc_ref[...], wo_ref[A_WIDTH + B_V:, :], preferred_element_type=F32)
    h_mid = h_ref[...] + mix
    o_ref[...] = h_mid
    hn_ref[...] = _rms_norm(h_mid, g_ref[...], NORM_EPS).astype(BF16)

    def gate_up(cols):
        up_cols = slice(D_FF + cols.start, D_FF + cols.stop)
        hn = hn_ref[...]
        return (jnp.dot(hn, wup_ref[:, cols], preferred_element_type=F32),
                jnp.dot(hn, wup_ref[:, up_cols], preferred_element_type=F32))

    def activate(cols, gate, up):
        prev = carry_ref[:, cols]
        carry_ref[:, cols] = gate[gate.shape[0] - CONV_HALO_ROWS:, :]
        gc = _causal_conv3(gate, prev, cw_ref[:, cols])
        act_ref[:, cols] = (gc / (1.0 + jnp.exp(-gc)) * up).astype(BF16)

    chunks = [slice(c0, min(c0 + FFN_CHUNK, D_FF)) for c0 in range(0, D_FF, FFN_CHUNK)]
    pending = gate_up(chunks[0])
    for c, cols in enumerate(chunks):
        nxt = gate_up(chunks[c + 1]) if c + 1 < len(chunks) else None
        activate(cols, *pending)
        pending = nxt

    out = o_ref[...] + jnp.dot(act_ref[...], wd_ref[...], preferred_element_type=F32)
    if final_norm:
        out = _rms_norm(out, fg_ref[...], NORM_EPS)
    o_ref[...] = out


def _mix_ffn(h, o_a, o_b, o_c, w_out, g, w_up, conv_w, w_down, final_g, layer, seq, final_norm):
    t = h.shape[0]
    rows = FFN_ROWS
    row_spec = lambda width: pl.BlockSpec((rows, width), lambda i: (i, 0))
    full = lambda shape: pl.BlockSpec(shape, lambda i: (0, 0))
    hbm = pl.BlockSpec(memory_space=pl.ANY)
    return pl.pallas_call(
        functools.partial(_mix_ffn_kernel, layer=layer, tiles_per_seq=seq // rows,
                          final_norm=final_norm),
        grid=(t // rows,),
        in_specs=[row_spec(D_MODEL), row_spec(A_WIDTH), row_spec(B_V), row_spec(C_WIDTH),
                  hbm, full((1, D_MODEL)), hbm, full((CONV_WIDTH, D_FF)), hbm, full((1, D_MODEL))],
        out_specs=row_spec(D_MODEL),
        out_shape=jax.ShapeDtypeStruct((t, D_MODEL), F32),
        scratch_shapes=([pltpu.VMEM((rows, D_MODEL), BF16),
                         pltpu.VMEM((rows, D_FF), BF16),
                         pltpu.VMEM((CONV_HALO_ROWS, D_FF), F32)]
                        + [pltpu.VMEM(w.shape[1:], BF16) for w in (w_out, w_up, w_down)]
                        + _weight_ring_scratch()),
        compiler_params=pltpu.CompilerParams(dimension_semantics=("arbitrary",),
                                             vmem_limit_bytes=VMEM_LIMIT_BYTES),
        name="mix_ffn",
    )(h, o_a, o_b, o_c, w_out, g, w_up, conv_w, w_down, final_g)


def kernel(x, positions, norm_mix_g, w_in, lambda_q1, lambda_k1, lambda_q2, lambda_k2, subln_g,
           conv_mix_w, w_out, norm_ffn_g, w_up, conv_ffn_w, w_down, final_g):
    batch, seq, d = x.shape
    depth = w_in.shape[0]
    assert d == D_MODEL and seq % ATT_ROWS == 0 and seq % IN_ROWS == 0 and seq % FFN_ROWS == 0
    t = batch * seq
    h = x.reshape(t, d)
    tables = _rope_tables(positions.reshape(t))
    bias_a = jnp.asarray(_dilated_bias_table(seq))
    bias_b = jnp.asarray(_causal_bias_table())
    row = lambda v: v.reshape(1, -1)

    for layer in range(depth):
        a_qkv, b_qkv, o_c = _in_proj(h, row(norm_mix_g[layer]), w_in, layer, tables,
                                     conv_mix_w[layer], seq)
        o_a = _attention_call(_dilated_attn_kernel, a_qkv, bias_a, [], seq,
                              n_pairs=A_HEADS // 2, q_off=0, k_off=2, v_off=4, name="dilated_attn")
        lam_init = 0.8 - 0.6 * math.exp(-0.3 * layer)
        o_b = _attention_call(
            functools.partial(_diff_attn_kernel, lam_init=lam_init), b_qkv, bias_b,
            [row(lambda_q1[layer]), row(lambda_k1[layer]), row(lambda_q2[layer]),
             row(lambda_k2[layer]), subln_g[layer].reshape(-1, 1)],
            seq, n_pairs=B_HEADS, q_off=0, k_off=B_HEADS, v_off=2 * B_HEADS, name="diff_attn")
        h = _mix_ffn(h, o_a, o_b, o_c, w_out, row(norm_ffn_g[layer]), w_up, conv_ffn_w[layer], w_down,
                     row(final_g), layer, seq, final_norm=(layer == depth - 1))
    return h.reshape(batch, seq, d)
```

```python
import functools
import math

import numpy as np
import jax
import jax.numpy as jnp
from jax import lax
from jax.experimental import pallas as pl
from jax.experimental.pallas import tpu as pltpu

D_MODEL = 1024
HEAD_DIM = 64
A_HEADS = 4
A_WIDTH = 256
B_HEADS = 4
B_QK = 512
B_V = 512
C_WIDTH = 256
IN_COLS = 3072
DILATED_PATTERNS = ((128, 1), (512, 4), (2048, 16))
CONV_WIDTH = 3
D_FF = 2816
ROPE_THETA = 500000.0
ROPE_DIMS = HEAD_DIM // 4
NORM_EPS = 1e-6
SUBLN_EPS = 1e-5

LANES = 128
CONV_HALO_ROWS = 8
MASK_BIAS = -1e30
LOG2_E = math.log2(math.e)
VMEM_LIMIT_BYTES = 56 * 1024 * 1024

F32 = jnp.float32
BF16 = jnp.bfloat16


def _rms_norm(x, g, eps):
    y = x * lax.rsqrt(jnp.mean(x * x, axis=-1, keepdims=True) + eps)
    return y * g


def _causal_conv3(p, prev, w):
    rows = lax.broadcasted_iota(jnp.int32, p.shape, 0)
    last = prev[CONV_HALO_ROWS - 1:CONV_HALO_ROWS, :]
    last2 = prev[CONV_HALO_ROWS - 2:CONV_HALO_ROWS - 1, :]
    p1 = jnp.where(rows == 0, last, pltpu.roll(p, 1, axis=0))
    p2 = jnp.where(rows == 0, last2, jnp.where(rows == 1, last, pltpu.roll(p, 2, axis=0)))
    return w[0:1, :] * p2 + w[1:2, :] * p1 + w[2:3, :] * p


ROPE_TOKENS_PER_ROW = LANES // ROPE_DIMS


def _rope_table_kernel(pos_ref, inv_ref, o_ref):
    ang = pos_ref[...].astype(F32) * inv_ref[...]
    lane = lax.broadcasted_iota(jnp.int32, ang.shape, 1) % ROPE_DIMS
    o_ref[...] = jnp.where(lane < ROPE_DIMS // 2, jnp.cos(ang), jnp.sin(ang))


def _rope_tables(pos):
    t = pos.shape[0]
    half = ROPE_DIMS // 2
    inv = ROPE_THETA ** (-(jnp.arange(half, dtype=F32) * 2.0 / ROPE_DIMS))
    inv_lane = jnp.tile(inv, LANES // half)[None, :]
    rows = t // ROPE_TOKENS_PER_ROW
    pos_rep = jnp.repeat(pos.reshape(rows, ROPE_TOKENS_PER_ROW), ROPE_DIMS, axis=1)
    compact = pl.pallas_call(
        _rope_table_kernel,
        grid=(1,),
        in_specs=[pl.BlockSpec((rows, LANES), lambda i: (0, 0)),
                  pl.BlockSpec((1, LANES), lambda i: (0, 0))],
        out_specs=pl.BlockSpec((rows, LANES), lambda i: (0, 0)),
        out_shape=jax.ShapeDtypeStruct((rows, LANES), F32),
        name="rope_tables",
    )(pos_rep, inv_lane)
    return compact


IN_ROWS = 1024


def _rope_lane_tables(cs):
    half = ROPE_DIMS // 2
    groups = cs.shape[0]
    shape = (groups * ROPE_TOKENS_PER_ROW, LANES)
    spread = jnp.broadcast_to(cs[:, None, :], (groups, ROPE_TOKENS_PER_ROW, LANES)).reshape(shape)
    token = lax.broadcasted_iota(jnp.int32, shape, 0) % ROPE_TOKENS_PER_ROW
    lane = lax.broadcasted_iota(jnp.int32, shape, 1) % HEAD_DIM
    cos_idx = token * ROPE_DIMS + lane % half
    cos = jnp.take_along_axis(spread, cos_idx, axis=1)
    sin = jnp.take_along_axis(spread, cos_idx + half, axis=1)
    return (jnp.where(lane < ROPE_DIMS, cos, 1.0),
            jnp.where((lane >= half) & (lane < ROPE_DIMS), sin, 0.0),
            jnp.where(lane < half, -sin, 0.0))


def _in_proj_kernel(h_ref, g_ref, w_hbm, cs_ref, cw_ref,
                    a_ref, b_ref, oc_ref, hn_ref, carry_ref, w_ref, w_stage, w_sem,
                    *, layer, tiles_per_seq):
    i = pl.program_id(0)

    @pl.when(i == 0)
    def _():
        _load_weights_bf16([(w_hbm, w_ref)], layer, w_stage, w_sem)

    hn_ref[...] = _rms_norm(h_ref[...], g_ref[...], NORM_EPS).astype(BF16)

    cos, sin_up, sin_dn = _rope_lane_tables(cs_ref[...])
    half = ROPE_DIMS // 2

    def proj(c0, width):
        return jnp.dot(hn_ref[...], w_ref[:, c0:c0 + width], preferred_element_type=F32)

    def rope(y):
        return (y * cos + pltpu.roll(y, half, axis=1) * sin_up
                + pltpu.roll(y, LANES - half, axis=1) * sin_dn)

    def rope_cols(c0, width, scale, out_ref, o0):
        y = proj(c0, width)
        for k in range(width // LANES):
            r = rope(y[:, k * LANES:(k + 1) * LANES])
            if scale != 1.0:
                r = r * scale
            out_ref[:, o0 + k * LANES:o0 + (k + 1) * LANES] = r.astype(BF16)

    scale = HEAD_DIM ** -0.5 * LOG2_E
    rope_cols(0, A_WIDTH, scale, a_ref, 0)
    rope_cols(A_WIDTH, A_WIDTH, 1.0, a_ref, A_WIDTH)
    a_ref[:, 2 * A_WIDTH:3 * A_WIDTH] = proj(2 * A_WIDTH, A_WIDTH).astype(BF16)
    b0 = 3 * A_WIDTH
    rope_cols(b0, B_QK, scale, b_ref, 0)
    rope_cols(b0 + B_QK, B_QK, 1.0, b_ref, B_QK)
    b_ref[:, 2 * B_QK:2 * B_QK + B_V] = proj(b0 + 2 * B_QK, B_V).astype(BF16)

    c0 = b0 + 2 * B_QK + B_V
    gate = proj(c0, C_WIDTH)
    prod = proj(c0 + C_WIDTH, C_WIDTH) * proj(c0 + 2 * C_WIDTH, C_WIDTH)

    @pl.when(i % tiles_per_seq == 0)
    def _():
        carry_ref[...] = jnp.zeros_like(carry_ref)

    prev = carry_ref[...]
    carry_ref[...] = prod[prod.shape[0] - CONV_HALO_ROWS:, :]
    oc_ref[...] = (gate * _causal_conv3(prod, prev, cw_ref[...])).astype(BF16)


WEIGHT_CHUNK = (256, 512)
WEIGHT_RING = 12


def _weight_ring_scratch():
    return [pltpu.VMEM((WEIGHT_RING,) + WEIGHT_CHUNK, F32), pltpu.SemaphoreType.DMA((WEIGHT_RING,))]


def _load_weights_bf16(jobs, layer, stage_ref, sem):
    cr, cc = WEIGHT_CHUNK
    chunks = [(w, dst, r0, c0) for w, dst in jobs
              for r0 in range(0, dst.shape[0], cr) for c0 in range(0, dst.shape[1], cc)]

    def copy(k):
        w, _, r0, c0 = chunks[k]
        slot = k % WEIGHT_RING
        return pltpu.make_async_copy(w.at[layer, r0:r0 + cr, c0:c0 + cc], stage_ref.at[slot],
                                     sem.at[slot])

    for k in range(min(WEIGHT_RING, len(chunks))):
        copy(k).start()
    for k, (_, dst, r0, c0) in enumerate(chunks):
        copy(k).wait()
        dst[r0:r0 + cr, c0:c0 + cc] = stage_ref[k % WEIGHT_RING].astype(BF16)
        if k + WEIGHT_RING < len(chunks):
            copy(k + WEIGHT_RING).start()


def _in_proj(h, g, w, layer, rope_cs, conv_w, seq):
    t = h.shape[0]
    rows = IN_ROWS
    row_spec = lambda width: pl.BlockSpec((rows, width), lambda i: (i, 0))
    full = lambda shape: pl.BlockSpec(shape, lambda i: (0, 0))
    return pl.pallas_call(
        functools.partial(_in_proj_kernel, layer=layer, tiles_per_seq=seq // rows),
        grid=(t // rows,),
        in_specs=[row_spec(D_MODEL), full((1, D_MODEL)), pl.BlockSpec(memory_space=pl.ANY),
                  pl.BlockSpec((rows // ROPE_TOKENS_PER_ROW, LANES), lambda i: (i, 0)),
                  full((CONV_WIDTH, C_WIDTH))],
        out_specs=[row_spec(3 * A_WIDTH), row_spec(2 * B_QK + B_V), row_spec(C_WIDTH)],
        out_shape=[jax.ShapeDtypeStruct((t, 3 * A_WIDTH), BF16),
                   jax.ShapeDtypeStruct((t, 2 * B_QK + B_V), BF16),
                   jax.ShapeDtypeStruct((t, C_WIDTH), BF16)],
        scratch_shapes=[pltpu.VMEM((rows, D_MODEL), BF16),
                        pltpu.VMEM((CONV_HALO_ROWS, C_WIDTH), F32),
                        pltpu.VMEM(w.shape[1:], BF16)] + _weight_ring_scratch(),
        compiler_params=pltpu.CompilerParams(dimension_semantics=("arbitrary",),
                                             vmem_limit_bytes=VMEM_LIMIT_BYTES),
        name="in_proj",
    )(h, g, w, rope_cs, conv_w)


ATT_ROWS = 512
SCORE_SLOTS = 2
ONES_ROWS = 16


def _dilated_bias_table(seq):
    u = np.arange(seq)[:, None]
    r = np.arange(ATT_ROWS)[None, :]
    delta = r + seq - ATT_ROWS - u
    count = np.zeros(delta.shape, np.int64)
    for window, dil in DILATED_PATTERNS:
        count += (delta >= 0) & (delta <= window) & (delta % dil == 0)
    return np.where(count > 0, np.log2(np.maximum(count, 1)), MASK_BIAS).astype(np.float32)


def _causal_bias_table():
    key = np.arange(ATT_ROWS)[:, None]
    query = np.arange(ATT_ROWS)[None, :]
    return np.where(key <= query, 0.0, MASK_BIAS).astype(np.float32)


def _pair_attention(q_ref, k_ref, v_ref, vt_ref, s_refs, block_bias, emit):
    rows = ATT_ROWS
    n_chunks = q_ref.shape[0] // rows
    vt_ref[0:LANES, :] = v_ref[...].astype(F32).T.astype(BF16)
    vt_ref[LANES:, :] = jnp.ones((ONES_ROWS, vt_ref.shape[1]), BF16)
    lane = lax.broadcasted_iota(jnp.int32, (rows, LANES), 1)

    def stacked_queries(c):
        q = q_ref[c * rows:(c + 1) * rows, :]
        zero = jnp.zeros_like(q)
        return jnp.concatenate([jnp.where(lane < HEAD_DIM, q, zero),
                                jnp.where(lane >= HEAD_DIM, q, zero)], axis=0)

    half = rows // 2
    nt_dims = (((1,), (1,)), ((), ()))

    def late_lanes(x):
        return jnp.concatenate([x[:, half:rows], x[:, rows + half:]], axis=1)

    def late_lanes_only(x, fill):
        pad = jnp.full((x.shape[0], half), fill, x.dtype)
        return jnp.concatenate([pad, x[:, :half], pad, x[:, half:]], axis=1)

    def sublane_max(s):
        return jnp.max(s.reshape(s.shape[0] // 8, 8, s.shape[1]), axis=0)

    def score_block(c, j, qs, col_max):
        slot = s_refs[c % SCORE_SLOTS]
        bias = block_bias(c, j)
        if j < c:
            s = lax.dot_general(k_ref[j * rows:(j + 1) * rows, :], qs, nt_dims,
                                preferred_element_type=F32)
            if bias is not None:
                s = s + jnp.concatenate([bias, bias], axis=1)
            slot[j * rows:(j + 1) * rows, :] = s
            blk_max = sublane_max(s)
        else:
            mid = j * rows + half
            s = lax.dot_general(k_ref[j * rows:mid, :], qs, nt_dims, preferred_element_type=F32)
            s = s + jnp.concatenate([bias[:half], bias[:half]], axis=1)
            slot[j * rows:mid, :] = s
            qs_late = jnp.concatenate([qs[half:rows], qs[rows + half:]], axis=0)
            s_late = lax.dot_general(k_ref[mid:mid + half, :], qs_late, nt_dims,
                                     preferred_element_type=F32)
            s_late = s_late + jnp.concatenate([bias[half:, half:], bias[half:, half:]], axis=1)
            slot[mid:mid + half, 0:rows] = s_late
            blk_max = jnp.maximum(sublane_max(s), late_lanes_only(sublane_max(s_late), MASK_BIAS))
        return blk_max if col_max is None else jnp.maximum(col_max, blk_max)

    def value_block(c, j, m, acc):
        slot = s_refs[c % SCORE_SLOTS]
        if j < c:
            p = jnp.exp2(slot[j * rows:(j + 1) * rows, :] - m).astype(BF16)
            part = jnp.dot(vt_ref[:, j * rows:(j + 1) * rows], p, preferred_element_type=F32)
            return part if acc is None else acc + part
        mid = j * rows + half
        p = jnp.exp2(slot[j * rows:mid, :] - m).astype(BF16)
        part = jnp.dot(vt_ref[:, j * rows:mid], p, preferred_element_type=F32)
        acc = part if acc is None else acc + part
        p_late = jnp.exp2(slot[mid:mid + half, 0:rows] - late_lanes(m)).astype(BF16)
        part = jnp.dot(vt_ref[:, mid:mid + half], p_late, preferred_element_type=F32)
        return acc + late_lanes_only(part, 0.0)

    qs = stacked_queries(0)
    col_max = score_block(0, 0, qs, None)
    acc = None
    for c in range(n_chunks):
        m = jnp.max(col_max, axis=0, keepdims=True)
        nxt = c + 1
        if nxt < n_chunks:
            qs = stacked_queries(nxt)
        acc = None
        col_max = None
        for j in range(nxt + 1):
            if nxt < n_chunks:
                col_max = score_block(nxt, j, qs, col_max)
            if j <= c:
                acc = value_block(c, j, m, acc)
        emit(c, acc[0:LANES], acc[LANES:LANES + 1])


def _dilated_attn_kernel(q_ref, k_ref, v_ref, bias_ref, o_ref, vt_ref, *s_refs):
    rows = ATT_ROWS
    seq = q_ref.shape[0]

    def block_bias(c, j):
        first = seq - (c + 1 - j) * rows
        return bias_ref[first:first + rows, :]

    def emit(c, o_t, l):
        o_t = o_t * (1.0 / l)
        o = jnp.concatenate([o_t[:HEAD_DIM, :rows], o_t[HEAD_DIM:, rows:]], axis=0)
        o_ref[c * rows:(c + 1) * rows, :] = o.T.astype(o_ref.dtype)

    _pair_attention(q_ref, k_ref, v_ref, vt_ref, s_refs, block_bias, emit)


def _diff_attn_kernel(q_ref, k_ref, v_ref, bias_ref, lq1_ref, lk1_ref, lq2_ref, lk2_ref, g_ref,
                      o_ref, vt_ref, *s_refs, lam_init):
    rows = ATT_ROWS
    lam = (jnp.exp(jnp.sum(lq1_ref[...] * lk1_ref[...], axis=1, keepdims=True))
           - jnp.exp(jnp.sum(lq2_ref[...] * lk2_ref[...], axis=1, keepdims=True)) + lam_init)
    gain = g_ref[...]

    def block_bias(c, j):
        return bias_ref[...] if j == c else None

    def emit(c, o_t, l):
        o_t = o_t * (1.0 / l)
        o = o_t[:, :rows] - lam * o_t[:, rows:]
        o = o * lax.rsqrt(jnp.mean(o * o, axis=0, keepdims=True) + SUBLN_EPS) * gain
        o_ref[c * rows:(c + 1) * rows, :] = (o * (1.0 - lam_init)).T.astype(o_ref.dtype)

    _pair_attention(q_ref, k_ref, v_ref, vt_ref, s_refs, block_bias, emit)


def _attention_call(body, qkv, bias, extra, seq, n_pairs, q_off, k_off, v_off, name):
    t = qkv.shape[0]
    batch = t // seq
    small = lambda a: pl.BlockSpec(a.shape, lambda b, p: (0,) * a.ndim)
    col = lambda off: pl.BlockSpec((seq, LANES), lambda b, p: (b, off + p))
    return pl.pallas_call(
        body,
        grid=(batch, n_pairs),
        in_specs=[col(q_off), col(k_off), col(v_off), small(bias)] + [small(a) for a in extra],
        out_specs=col(0),
        out_shape=jax.ShapeDtypeStruct((t, n_pairs * LANES), BF16),
        scratch_shapes=([pltpu.VMEM((LANES + ONES_ROWS, seq), BF16)]
                        + [pltpu.VMEM((seq, 2 * ATT_ROWS), F32)] * SCORE_SLOTS),
        compiler_params=pltpu.CompilerParams(dimension_semantics=("arbitrary", "arbitrary"),
                                             vmem_limit_bytes=VMEM_LIMIT_BYTES),
        name=name,
    )(qkv, qkv, qkv, bias, *extra)


FFN_ROWS = 512
FFN_CHUNK = 512


def _mix_ffn_kernel(h_ref, oa_ref, ob_ref, oc_ref, wo_hbm, g_ref, wup_hbm, cw_ref, wd_hbm, fg_ref,
                    o_ref, hn_ref, act_ref, carry_ref, wo_ref, wup_ref, wd_ref, w_stage, w_sem,
                    *, layer, tiles_per_seq, final_norm):
    i = pl.program_id(0)

    @pl.when(i == 0)
    def _():
        _load_weights_bf16([(wo_hbm, wo_ref), (wup_hbm, wup_ref), (wd_hbm, wd_ref)], layer,
                           w_stage, w_sem)

    @pl.when(i % tiles_per_seq == 0)
    def _():
        carry_ref[...] = jnp.zeros_like(carry_ref)

    mix = jnp.dot(oa_ref[...], wo_ref[0:A_WIDTH, :], preferred_element_type=F32)
    mix += jnp.dot(ob_ref[...], wo_ref[A_WIDTH:A_WIDTH + B_V, :], preferred_element_type=F32)
    mix += jnp.dot(oc_ref[...], wo_ref[A_WIDTH + B_V:, :], preferred_element_type=F32)
    h_mid = h_ref[...] + mix
    o_ref[...] = h_mid
    hn_ref[...] = _rms_norm(h_mid, g_ref[...], NORM_EPS).astype(BF16)

    def gate_up(cols):
        up_cols = slice(D_FF + cols.start, D_FF + cols.stop)
        hn = hn_ref[...]
        return (jnp.dot(hn, wup_ref[:, cols], preferred_element_type=F32),
                jnp.dot(hn, wup_ref[:, up_cols], preferred_element_type=F32))

    def activate(cols, gate, up):
        prev = carry_ref[:, cols]
        carry_ref[:, cols] = gate[gate.shape[0] - CONV_HALO_ROWS:, :]
        gc = _causal_conv3(gate, prev, cw_ref[:, cols])
        act_ref[:, cols] = (gc / (1.0 + jnp.exp(-gc)) * up).astype(BF16)

    chunks = [slice(c0, min(c0 + FFN_CHUNK, D_FF)) for c0 in range(0, D_FF, FFN_CHUNK)]
    pending = gate_up(chunks[0])
    for c, cols in enumerate(chunks):
        nxt = gate_up(chunks[c + 1]) if c + 1 < len(chunks) else None
        activate(cols, *pending)
        pending = nxt

    out = o_ref[...] + jnp.dot(act_ref[...], wd_ref[...], preferred_element_type=F32)
    if final_norm:
        out = _rms_norm(out, fg_ref[...], NORM_EPS)
    o_ref[...] = out


def _mix_ffn(h, o_a, o_b, o_c, w_out, g, w_up, conv_w, w_down, final_g, layer, seq, final_norm):
    t = h.shape[0]
    rows = FFN_ROWS
    row_spec = lambda width: pl.BlockSpec((rows, width), lambda i: (i, 0))
    full = lambda shape: pl.BlockSpec(shape, lambda i: (0, 0))
    hbm = pl.BlockSpec(memory_space=pl.ANY)
    return pl.pallas_call(
        functools.partial(_mix_ffn_kernel, layer=layer, tiles_per_seq=seq // rows,
                          final_norm=final_norm),
        grid=(t // rows,),
        in_specs=[row_spec(D_MODEL), row_spec(A_WIDTH), row_spec(B_V), row_spec(C_WIDTH),
                  hbm, full((1, D_MODEL)), hbm, full((CONV_WIDTH, D_FF)), hbm, full((1, D_MODEL))],
        out_specs=row_spec(D_MODEL),
        out_shape=jax.ShapeDtypeStruct((t, D_MODEL), F32),
        scratch_shapes=([pltpu.VMEM((rows, D_MODEL), BF16),
                         pltpu.VMEM((rows, D_FF), BF16),
                         pltpu.VMEM((CONV_HALO_ROWS, D_FF), F32)]
                        + [pltpu.VMEM(w.shape[1:], BF16) for w in (w_out, w_up, w_down)]
                        + _weight_ring_scratch()),
        compiler_params=pltpu.CompilerParams(dimension_semantics=("arbitrary",),
                                             vmem_limit_bytes=VMEM_LIMIT_BYTES),
        name="mix_ffn",
    )(h, o_a, o_b, o_c, w_out, g, w_up, conv_w, w_down, final_g)


def kernel(x, positions, norm_mix_g, w_in, lambda_q1, lambda_k1, lambda_q2, lambda_k2, subln_g,
           conv_mix_w, w_out, norm_ffn_g, w_up, conv_ffn_w, w_down, final_g):
    batch, seq, d = x.shape
    depth = w_in.shape[0]
    assert d == D_MODEL and seq % ATT_ROWS == 0 and seq % IN_ROWS == 0 and seq % FFN_ROWS == 0
    t = batch * seq
    h = x.reshape(t, d)
    tables = _rope_tables(positions.reshape(t))
    bias_a = jnp.asarray(_dilated_bias_table(seq))
    bias_b = jnp.asarray(_causal_bias_table())
    row = lambda v: v.reshape(1, -1)

    for layer in range(depth):
        a_qkv, b_qkv, o_c = _in_proj(h, row(norm_mix_g[layer]), w_in, layer, tables,
                                     conv_mix_w[layer], seq)
        o_a = _attention_call(_dilated_attn_kernel, a_qkv, bias_a, [], seq,
                              n_pairs=A_HEADS // 2, q_off=0, k_off=2, v_off=4, name="dilated_attn")
        lam_init = 0.8 - 0.6 * math.exp(-0.3 * layer)
        o_b = _attention_call(
            functools.partial(_diff_attn_kernel, lam_init=lam_init), b_qkv, bias_b,
            [row(lambda_q1[layer]), row(lambda_k1[layer]), row(lambda_q2[layer]),
             row(lambda_k2[layer]), subln_g[layer].reshape(-1, 1)],
            seq, n_pairs=B_HEADS, q_off=0, k_off=B_HEADS, v_off=2 * B_HEADS, name="diff_attn")
        h = _mix_ffn(h, o_a, o_b, o_c, w_out, row(norm_ffn_g[layer]), w_up, conv_ffn_w[layer], w_down,
                     row(final_g), layer, seq, final_norm=(layer == depth - 1))
    return h.reshape(batch, seq, d)
```

```python
import functools
import math

import numpy as np
import jax
import jax.numpy as jnp
from jax import lax
from jax.experimental import pallas as pl
from jax.experimental.pallas import tpu as pltpu

D_MODEL = 1024
HEAD_DIM = 64
A_HEADS = 4
A_WIDTH = 256
B_HEADS = 4
B_QK = 512
B_V = 512
C_WIDTH = 256
IN_COLS = 3072
DILATED_PATTERNS = ((128, 1), (512, 4), (2048, 16))
CONV_WIDTH = 3
D_FF = 2816
ROPE_THETA = 500000.0
ROPE_DIMS = HEAD_DIM // 4
NORM_EPS = 1e-6
SUBLN_EPS = 1e-5

LANES = 128
CONV_HALO_ROWS = 8
MASK_BIAS = -1e30
LOG2_E = math.log2(math.e)
VMEM_LIMIT_BYTES = 56 * 1024 * 1024

F32 = jnp.float32
BF16 = jnp.bfloat16


def _rms_norm(x, g, eps):
    y = x * lax.rsqrt(jnp.mean(x * x, axis=-1, keepdims=True) + eps)
    return y * g


def _causal_conv3(p, prev, w):
    rows = lax.broadcasted_iota(jnp.int32, p.shape, 0)
    last = prev[CONV_HALO_ROWS - 1:CONV_HALO_ROWS, :]
    last2 = prev[CONV_HALO_ROWS - 2:CONV_HALO_ROWS - 1, :]
    p1 = jnp.where(rows == 0, last, pltpu.roll(p, 1, axis=0))
    p2 = jnp.where(rows == 0, last2, jnp.where(rows == 1, last, pltpu.roll(p, 2, axis=0)))
    return w[0:1, :] * p2 + w[1:2, :] * p1 + w[2:3, :] * p


ROPE_TOKENS_PER_ROW = LANES // ROPE_DIMS


def _rope_table_kernel(pos_ref, inv_ref, o_ref):
    ang = pos_ref[...].astype(F32) * inv_ref[...]
    lane = lax.broadcasted_iota(jnp.int32, ang.shape, 1) % ROPE_DIMS
    o_ref[...] = jnp.where(lane < ROPE_DIMS // 2, jnp.cos(ang), jnp.sin(ang))


def _rope_tables(pos):
    t = pos.shape[0]
    half = ROPE_DIMS // 2
    inv = ROPE_THETA ** (-(jnp.arange(half, dtype=F32) * 2.0 / ROPE_DIMS))
    inv_lane = jnp.tile(inv, LANES // half)[None, :]
    rows = t // ROPE_TOKENS_PER_ROW
    pos_rep = jnp.repeat(pos.reshape(rows, ROPE_TOKENS_PER_ROW), ROPE_DIMS, axis=1)
    compact = pl.pallas_call(
        _rope_table_kernel,
        grid=(1,),
        in_specs=[pl.BlockSpec((rows, LANES), lambda i: (0, 0)),
                  pl.BlockSpec((1, LANES), lambda i: (0, 0))],
        out_specs=pl.BlockSpec((rows, LANES), lambda i: (0, 0)),
        out_shape=jax.ShapeDtypeStruct((rows, LANES), F32),
        name="rope_tables",
    )(pos_rep, inv_lane)
    return compact


IN_ROWS = 1024


def _rope_lane_tables(cs):
    half = ROPE_DIMS // 2
    groups = cs.shape[0]
    shape = (groups * ROPE_TOKENS_PER_ROW, LANES)
    spread = jnp.broadcast_to(cs[:, None, :], (groups, ROPE_TOKENS_PER_ROW, LANES)).reshape(shape)
    token = lax.broadcasted_iota(jnp.int32, shape, 0) % ROPE_TOKENS_PER_ROW
    lane = lax.broadcasted_iota(jnp.int32, shape, 1) % HEAD_DIM
    cos_idx = token * ROPE_DIMS + lane % half
    cos = jnp.take_along_axis(spread, cos_idx, axis=1)
    sin = jnp.take_along_axis(spread, cos_idx + half, axis=1)
    return (jnp.where(lane < ROPE_DIMS, cos, 1.0),
            jnp.where((lane >= half) & (lane < ROPE_DIMS), sin, 0.0),
            jnp.where(lane < half, -sin, 0.0))


def _in_proj_kernel(h_ref, g_ref, w_hbm, cs_ref, cw_ref,
                    a_ref, b_ref, oc_ref, hn_ref, carry_ref, w_ref, w_stage, w_sem,
                    *, layer, tiles_per_seq):
    i = pl.program_id(0)

    @pl.when(i == 0)
    def _():
        _load_weights_bf16([(w_hbm, w_ref)], layer, w_stage, w_sem)

    hn_ref[...] = _rms_norm(h_ref[...], g_ref[...], NORM_EPS).astype(BF16)

    cos, sin_up, sin_dn = _rope_lane_tables(cs_ref[...])
    half = ROPE_DIMS // 2

    def proj(c0, width):
        return jnp.dot(hn_ref[...], w_ref[:, c0:c0 + width], preferred_element_type=F32)

    def rope(y):
        return (y * cos + pltpu.roll(y, half, axis=1) * sin_up
                + pltpu.roll(y, LANES - half, axis=1) * sin_dn)

    def rope_cols(c0, width, scale, out_ref, o0):
        y = proj(c0, width)
        for k in range(width // LANES):
            r = rope(y[:, k * LANES:(k + 1) * LANES])
            if scale != 1.0:
                r = r * scale
            out_ref[:, o0 + k * LANES:o0 + (k + 1) * LANES] = r.astype(BF16)

    scale = HEAD_DIM ** -0.5 * LOG2_E
    rope_cols(0, A_WIDTH, scale, a_ref, 0)
    rope_cols(A_WIDTH, A_WIDTH, 1.0, a_ref, A_WIDTH)
    a_ref[:, 2 * A_WIDTH:3 * A_WIDTH] = proj(2 * A_WIDTH, A_WIDTH).astype(BF16)
    b0 = 3 * A_WIDTH
    rope_cols(b0, B_QK, scale, b_ref, 0)
    rope_cols(b0 + B_QK, B_QK, 1.0, b_ref, B_QK)
    b_ref[:, 2 * B_QK:2 * B_QK + B_V] = proj(b0 + 2 * B_QK, B_V).astype(BF16)

    c0 = b0 + 2 * B_QK + B_V
    gate = proj(c0, C_WIDTH)
    prod = proj(c0 + C_WIDTH, C_WIDTH) * proj(c0 + 2 * C_WIDTH, C_WIDTH)

    @pl.when(i % tiles_per_seq == 0)
    def _():
        carry_ref[...] = jnp.zeros_like(carry_ref)

    prev = carry_ref[...]
    carry_ref[...] = prod[prod.shape[0] - CONV_HALO_ROWS:, :]
    oc_ref[...] = (gate * _causal_conv3(prod, prev, cw_ref[...])).astype(BF16)


WEIGHT_CHUNK = (256, 512)
WEIGHT_RING = 12


def _weight_ring_scratch():
    return [pltpu.VMEM((WEIGHT_RING,) + WEIGHT_CHUNK, F32), pltpu.SemaphoreType.DMA((WEIGHT_RING,))]


def _load_weights_bf16(jobs, layer, stage_ref, sem):
    cr, cc = WEIGHT_CHUNK
    chunks = [(w, dst, r0, c0) for w, dst in jobs
              for r0 in range(0, dst.shape[0], cr) for c0 in range(0, dst.shape[1], cc)]

    def copy(k):
        w, _, r0, c0 = chunks[k]
        slot = k % WEIGHT_RING
        return pltpu.make_async_copy(w.at[layer, r0:r0 + cr, c0:c0 + cc], stage_ref.at[slot],
                                     sem.at[slot])

    for k in range(min(WEIGHT_RING, len(chunks))):
        copy(k).start()
    for k, (_, dst, r0, c0) in enumerate(chunks):
        copy(k).wait()
        dst[r0:r0 + cr, c0:c0 + cc] = stage_ref[k % WEIGHT_RING].astype(BF16)
        if k + WEIGHT_RING < len(chunks):
            copy(k + WEIGHT_RING).start()


def _in_proj(h, g, w, layer, rope_cs, conv_w, seq):
    t = h.shape[0]
    rows = IN_ROWS
    row_spec = lambda width: pl.BlockSpec((rows, width), lambda i: (i, 0))
    full = lambda shape: pl.BlockSpec(shape, lambda i: (0, 0))
    return pl.pallas_call(
        functools.partial(_in_proj_kernel, layer=layer, tiles_per_seq=seq // rows),
        grid=(t // rows,),
        in_specs=[row_spec(D_MODEL), full((1, D_MODEL)), pl.BlockSpec(memory_space=pl.ANY),
                  pl.BlockSpec((rows // ROPE_TOKENS_PER_ROW, LANES), lambda i: (i, 0)),
                  full((CONV_WIDTH, C_WIDTH))],
        out_specs=[row_spec(3 * A_WIDTH), row_spec(2 * B_QK + B_V), row_spec(C_WIDTH)],
        out_shape=[jax.ShapeDtypeStruct((t, 3 * A_WIDTH), BF16),
                   jax.ShapeDtypeStruct((t, 2 * B_QK + B_V), BF16),
                   jax.ShapeDtypeStruct((t, C_WIDTH), BF16)],
        scratch_shapes=[pltpu.VMEM((rows, D_MODEL), BF16),
                        pltpu.VMEM((CONV_HALO_ROWS, C_WIDTH), F32),
                        pltpu.VMEM(w.shape[1:], BF16)] + _weight_ring_scratch(),
        compiler_params=pltpu.CompilerParams(dimension_semantics=("arbitrary",),
                                             vmem_limit_bytes=VMEM_LIMIT_BYTES),
        name="in_proj",
    )(h, g, w, rope_cs, conv_w)


ATT_ROWS = 512
SCORE_SLOTS = 2
ONES_ROWS = 16


def _dilated_bias_table(seq):
    u = np.arange(seq)[:, None]
    r = np.arange(ATT_ROWS)[None, :]
    delta = r + seq - ATT_ROWS - u
    count = np.zeros(delta.shape, np.int64)
    for window, dil in DILATED_PATTERNS:
        count += (delta >= 0) & (delta <= window) & (delta % dil == 0)
    return np.where(count > 0, np.log2(np.maximum(count, 1)), MASK_BIAS).astype(np.float32)


def _causal_bias_table():
    key = np.arange(ATT_ROWS)[:, None]
    query = np.arange(ATT_ROWS)[None, :]
    return np.where(key <= query, 0.0, MASK_BIAS).astype(np.float32)


def _pair_attention(q_ref, k_ref, v_ref, vt_ref, s_refs, block_bias, emit):
    rows = ATT_ROWS
    n_chunks = q_ref.shape[0] // rows
    vt_ref[0:LANES, :] = v_ref[...].astype(F32).T.astype(BF16)
    vt_ref[LANES:, :] = jnp.ones((ONES_ROWS, vt_ref.shape[1]), BF16)
    lane = lax.broadcasted_iota(jnp.int32, (rows, LANES), 1)

    def stacked_queries(c):
        q = q_ref[c * rows:(c + 1) * rows, :]
        zero = jnp.zeros_like(q)
        return jnp.concatenate([jnp.where(lane < HEAD_DIM, q, zero),
                                jnp.where(lane >= HEAD_DIM, q, zero)], axis=0)

    half = rows // 2
    nt_dims = (((1,), (1,)), ((), ()))

    def late_lanes(x):
        return jnp.concatenate([x[:, half:rows], x[:, rows + half:]], axis=1)

    def late_lanes_only(x, fill):
        pad = jnp.full((x.shape[0], half), fill, x.dtype)
        return jnp.concatenate([pad, x[:, :half], pad, x[:, half:]], axis=1)

    def sublane_max(s):
        return jnp.max(s.reshape(s.shape[0] // 8, 8, s.shape[1]), axis=0)

    def score_block(c, j, qs, col_max):
        slot = s_refs[c % SCORE_SLOTS]
        bias = block_bias(c, j)
        if j < c:
            s = lax.dot_general(k_ref[j * rows:(j + 1) * rows, :], qs, nt_dims,
                                preferred_element_type=F32)
            if bias is not None:
                s = s + jnp.concatenate([bias, bias], axis=1)
            slot[j * rows:(j + 1) * rows, :] = s
            blk_max = sublane_max(s)
        else:
            mid = j * rows + half
            s = lax.dot_general(k_ref[j * rows:mid, :], qs, nt_dims, preferred_element_type=F32)
            s = s + jnp.concatenate([bias[:half], bias[:half]], axis=1)
            slot[j * rows:mid, :] = s
            qs_late = jnp.concatenate([qs[half:rows], qs[rows + half:]], axis=0)
            s_late = lax.dot_general(k_ref[mid:mid + half, :], qs_late, nt_dims,
                                     preferred_element_type=F32)
            s_late = s_late + jnp.concatenate([bias[half:, half:], bias[half:, half:]], axis=1)
            slot[mid:mid + half, 0:rows] = s_late
            blk_max = jnp.maximum(sublane_max(s), late_lanes_only(sublane_max(s_late), MASK_BIAS))
        return blk_max if col_max is None else jnp.maximum(col_max, blk_max)

    def value_block(c, j, m, acc):
        slot = s_refs[c % SCORE_SLOTS]
        if j < c:
            p = jnp.exp2(slot[j * rows:(j + 1) * rows, :] - m).astype(BF16)
            part = jnp.dot(vt_ref[:, j * rows:(j + 1) * rows], p, preferred_element_type=F32)
            return part if acc is None else acc + part
        mid = j * rows + half
        p = jnp.exp2(slot[j * rows:mid, :] - m).astype(BF16)
        part = jnp.dot(vt_ref[:, j * rows:mid], p, preferred_element_type=F32)
        acc = part if acc is None else acc + part
        p_late = jnp.exp2(slot[mid:mid + half, 0:rows] - late_lanes(m)).astype(BF16)
        part = jnp.dot(vt_ref[:, mid:mid + half], p_late, preferred_element_type=F32)
        return acc + late_lanes_only(part, 0.0)

    def score_pass(c):
        qs = stacked_queries(c)
        col_max = None
        for j in range(c + 1):
            col_max = score_block(c, j, qs, col_max)
        return col_max

    col_max = score_pass(0)
    for c in range(n_chunks):
        m = jnp.max(col_max, axis=0, keepdims=True)
        if c + 1 < n_chunks:
            col_max = score_pass(c + 1)
        acc = None
        for j in range(c + 1):
            acc = value_block(c, j, m, acc)
        emit(c, acc[0:LANES], acc[LANES:LANES + 1])


def _dilated_attn_kernel(q_ref, k_ref, v_ref, bias_ref, o_ref, vt_ref, *s_refs):
    rows = ATT_ROWS
    seq = q_ref.shape[0]

    def block_bias(c, j):
        first = seq - (c + 1 - j) * rows
        return bias_ref[first:first + rows, :]

    def emit(c, o_t, l):
        o_t = o_t * (1.0 / l)
        o = jnp.concatenate([o_t[:HEAD_DIM, :rows], o_t[HEAD_DIM:, rows:]], axis=0)
        o_ref[c * rows:(c + 1) * rows, :] = o.T.astype(o_ref.dtype)

    _pair_attention(q_ref, k_ref, v_ref, vt_ref, s_refs, block_bias, emit)


def _diff_attn_kernel(q_ref, k_ref, v_ref, bias_ref, lq1_ref, lk1_ref, lq2_ref, lk2_ref, g_ref,
                      o_ref, vt_ref, *s_refs, lam_init):
    rows = ATT_ROWS
    lam = (jnp.exp(jnp.sum(lq1_ref[...] * lk1_ref[...], axis=1, keepdims=True))
           - jnp.exp(jnp.sum(lq2_ref[...] * lk2_ref[...], axis=1, keepdims=True)) + lam_init)
    gain = g_ref[...]

    def block_bias(c, j):
        return bias_ref[...] if j == c else None

    def emit(c, o_t, l):
        o_t = o_t * (1.0 / l)
        o = o_t[:, :rows] - lam * o_t[:, rows:]
        o = o * lax.rsqrt(jnp.mean(o * o, axis=0, keepdims=True) + SUBLN_EPS) * gain
        o_ref[c * rows:(c + 1) * rows, :] = (o * (1.0 - lam_init)).T.astype(o_ref.dtype)

    _pair_attention(q_ref, k_ref, v_ref, vt_ref, s_refs, block_bias, emit)


def _attention_call(body, qkv, bias, extra, seq, n_pairs, q_off, k_off, v_off, name):
    t = qkv.shape[0]
    batch = t // seq
    small = lambda a: pl.BlockSpec(a.shape, lambda b, p: (0,) * a.ndim)
    col = lambda off: pl.BlockSpec((seq, LANES), lambda b, p: (b, off + p))
    return pl.pallas_call(
        body,
        grid=(batch, n_pairs),
        in_specs=[col(q_off), col(k_off), col(v_off), small(bias)] + [small(a) for a in extra],
        out_specs=col(0),
        out_shape=jax.ShapeDtypeStruct((t, n_pairs * LANES), BF16),
        scratch_shapes=([pltpu.VMEM((LANES + ONES_ROWS, seq), BF16)]
                        + [pltpu.VMEM((seq, 2 * ATT_ROWS), F32)] * SCORE_SLOTS),
        compiler_params=pltpu.CompilerParams(dimension_semantics=("arbitrary", "arbitrary"),
                                             vmem_limit_bytes=VMEM_LIMIT_BYTES),
        name=name,
    )(qkv, qkv, qkv, bias, *extra)


FFN_ROWS = 512
FFN_CHUNK = 1024


def _mix_ffn_kernel(h_ref, oa_ref, ob_ref, oc_ref, wo_hbm, g_ref, wup_hbm, cw_ref, wd_hbm, fg_ref,
                    o_ref, hn_ref, act_ref, carry_ref, wo_ref, wup_ref, wd_ref, w_stage, w_sem,
                    *, layer, tiles_per_seq, final_norm):
    i = pl.program_id(0)

    @pl.when(i == 0)
    def _():
        _load_weights_bf16([(wo_hbm, wo_ref), (wup_hbm, wup_ref), (wd_hbm, wd_ref)], layer,
                           w_stage, w_sem)

    @pl.when(i % tiles_per_seq == 0)
    def _():
        carry_ref[...] = jnp.zeros_like(carry_ref)

    mixed = jnp.concatenate([oa_ref[...], ob_ref[...], oc_ref[...]], axis=1)
    h_mid = h_ref[...] + jnp.dot(mixed, wo_ref[...], preferred_element_type=F32)
    o_ref[...] = h_mid
    hn_ref[...] = _rms_norm(h_mid, g_ref[...], NORM_EPS).astype(BF16)

    def gate_up(cols):
        up_cols = slice(D_FF + cols.start, D_FF + cols.stop)
        hn = hn_ref[...]
        return (jnp.dot(hn, wup_ref[:, cols], preferred_element_type=F32),
                jnp.dot(hn, wup_ref[:, up_cols], preferred_element_type=F32))

    def activate(cols, gate, up):
        prev = carry_ref[:, cols]
        carry_ref[:, cols] = gate[gate.shape[0] - CONV_HALO_ROWS:, :]
        gc = _causal_conv3(gate, prev, cw_ref[:, cols])
        act_ref[:, cols] = (gc / (1.0 + jnp.exp(-gc)) * up).astype(BF16)

    chunks = [slice(c0, min(c0 + FFN_CHUNK, D_FF)) for c0 in range(0, D_FF, FFN_CHUNK)]
    pending = gate_up(chunks[0])
    for c, cols in enumerate(chunks):
        nxt = gate_up(chunks[c + 1]) if c + 1 < len(chunks) else None
        activate(cols, *pending)
        pending = nxt

    out = o_ref[...] + jnp.dot(act_ref[...], wd_ref[...], preferred_element_type=F32)
    if final_norm:
        out = _rms_norm(out, fg_ref[...], NORM_EPS)
    o_ref[...] = out


def _mix_ffn(h, o_a, o_b, o_c, w_out, g, w_up, conv_w, w_down, final_g, layer, seq, final_norm):
    t = h.shape[0]
    rows = FFN_ROWS
    row_spec = lambda width: pl.BlockSpec((rows, width), lambda i: (i, 0))
    full = lambda shape: pl.BlockSpec(shape, lambda i: (0, 0))
    hbm = pl.BlockSpec(memory_space=pl.ANY)
    return pl.pallas_call(
        functools.partial(_mix_ffn_kernel, layer=layer, tiles_per_seq=seq // rows,
                          final_norm=final_norm),
        grid=(t // rows,),
        in_specs=[row_spec(D_MODEL), row_spec(A_WIDTH), row_spec(B_V), row_spec(C_WIDTH),
                  hbm, full((1, D_MODEL)), hbm, full((CONV_WIDTH, D_FF)), hbm, full((1, D_MODEL))],
        out_specs=row_spec(D_MODEL),
        out_shape=jax.ShapeDtypeStruct((t, D_MODEL), F32),
        scratch_shapes=([pltpu.VMEM((rows, D_MODEL), BF16),
                         pltpu.VMEM((rows, D_FF), BF16),
                         pltpu.VMEM((CONV_HALO_ROWS, D_FF), F32)]
                        + [pltpu.VMEM(w.shape[1:], BF16) for w in (w_out, w_up, w_down)]
                        + _weight_ring_scratch()),
        compiler_params=pltpu.CompilerParams(dimension_semantics=("arbitrary",),
                                             vmem_limit_bytes=VMEM_LIMIT_BYTES),
        name="mix_ffn",
    )(h, o_a, o_b, o_c, w_out, g, w_up, conv_w, w_down, final_g)


def kernel(x, positions, norm_mix_g, w_in, lambda_q1, lambda_k1, lambda_q2, lambda_k2, subln_g,
           conv_mix_w, w_out, norm_ffn_g, w_up, conv_ffn_w, w_down, final_g):
    batch, seq, d = x.shape
    depth = w_in.shape[0]
    assert d == D_MODEL and seq % ATT_ROWS == 0 and seq % IN_ROWS == 0 and seq % FFN_ROWS == 0
    t = batch * seq
    h = x.reshape(t, d)
    tables = _rope_tables(positions.reshape(t))
    bias_a = jnp.asarray(_dilated_bias_table(seq))
    bias_b = jnp.asarray(_causal_bias_table())
    row = lambda v: v.reshape(1, -1)

    for layer in range(depth):
        a_qkv, b_qkv, o_c = _in_proj(h, row(norm_mix_g[layer]), w_in, layer, tables,
                                     conv_mix_w[layer], seq)
        o_a = _attention_call(_dilated_attn_kernel, a_qkv, bias_a, [], seq,
                              n_pairs=A_HEADS // 2, q_off=0, k_off=2, v_off=4, name="dilated_attn")
        lam_init = 0.8 - 0.6 * math.exp(-0.3 * layer)
        o_b = _attention_call(
            functools.partial(_diff_attn_kernel, lam_init=lam_init), b_qkv, bias_b,
            [row(lambda_q1[layer]), row(lambda_k1[layer]), row(lambda_q2[layer]),
             row(lambda_k2[layer]), subln_g[layer].reshape(-1, 1)],
            seq, n_pairs=B_HEADS, q_off=0, k_off=B_HEADS, v_off=2 * B_HEADS, name="diff_attn")
        h = _mix_ffn(h, o_a, o_b, o_c, w_out, row(norm_ffn_g[layer]), w_up, conv_ffn_w[layer], w_down,
                     row(final_g), layer, seq, final_norm=(layer == depth - 1))
    return h.reshape(batch, seq, d)
```

```python
import functools
import math

import numpy as np
import jax
import jax.numpy as jnp
from jax import lax
from jax.experimental import pallas as pl
from jax.experimental.pallas import tpu as pltpu

D_MODEL = 1024
HEAD_DIM = 64
A_HEADS = 4
A_WIDTH = 256
B_HEADS = 4
B_QK = 512
B_V = 512
C_WIDTH = 256
IN_COLS = 3072
DILATED_PATTERNS = ((128, 1), (512, 4), (2048, 16))
CONV_WIDTH = 3
D_FF = 2816
ROPE_THETA = 500000.0
ROPE_DIMS = HEAD_DIM // 4
NORM_EPS = 1e-6
SUBLN_EPS = 1e-5

LANES = 128
CONV_HALO_ROWS = 8
MASK_BIAS = -1e30
LOG2_E = math.log2(math.e)
VMEM_LIMIT_BYTES = 56 * 1024 * 1024

F32 = jnp.float32
BF16 = jnp.bfloat16


def _rms_norm(x, g, eps):
    y = x * lax.rsqrt(jnp.mean(x * x, axis=-1, keepdims=True) + eps)
    return y * g


def _causal_conv3(p, prev, w):
    rows = lax.broadcasted_iota(jnp.int32, p.shape, 0)
    last = prev[CONV_HALO_ROWS - 1:CONV_HALO_ROWS, :]
    last2 = prev[CONV_HALO_ROWS - 2:CONV_HALO_ROWS - 1, :]
    p1 = jnp.where(rows == 0, last, pltpu.roll(p, 1, axis=0))
    p2 = jnp.where(rows == 0, last2, jnp.where(rows == 1, last, pltpu.roll(p, 2, axis=0)))
    return w[0:1, :] * p2 + w[1:2, :] * p1 + w[2:3, :] * p


ROPE_TOKENS_PER_ROW = LANES // ROPE_DIMS


def _rope_table_kernel(pos_ref, inv_ref, o_ref):
    ang = pos_ref[...].astype(F32) * inv_ref[...]
    lane = lax.broadcasted_iota(jnp.int32, ang.shape, 1) % ROPE_DIMS
    o_ref[...] = jnp.where(lane < ROPE_DIMS // 2, jnp.cos(ang), jnp.sin(ang))


def _rope_tables(pos):
    t = pos.shape[0]
    half = ROPE_DIMS // 2
    inv = ROPE_THETA ** (-(jnp.arange(half, dtype=F32) * 2.0 / ROPE_DIMS))
    inv_lane = jnp.tile(inv, LANES // half)[None, :]
    rows = t // ROPE_TOKENS_PER_ROW
    pos_rep = jnp.repeat(pos.reshape(rows, ROPE_TOKENS_PER_ROW), ROPE_DIMS, axis=1)
    compact = pl.pallas_call(
        _rope_table_kernel,
        grid=(1,),
        in_specs=[pl.BlockSpec((rows, LANES), lambda i: (0, 0)),
                  pl.BlockSpec((1, LANES), lambda i: (0, 0))],
        out_specs=pl.BlockSpec((rows, LANES), lambda i: (0, 0)),
        out_shape=jax.ShapeDtypeStruct((rows, LANES), F32),
        name="rope_tables",
    )(pos_rep, inv_lane)
    return compact


IN_ROWS = 1024


def _rope_lane_tables(cs):
    half = ROPE_DIMS // 2
    groups = cs.shape[0]
    shape = (groups * ROPE_TOKENS_PER_ROW, LANES)
    spread = jnp.broadcast_to(cs[:, None, :], (groups, ROPE_TOKENS_PER_ROW, LANES)).reshape(shape)
    token = lax.broadcasted_iota(jnp.int32, shape, 0) % ROPE_TOKENS_PER_ROW
    lane = lax.broadcasted_iota(jnp.int32, shape, 1) % HEAD_DIM
    cos_idx = token * ROPE_DIMS + lane % half
    cos = jnp.take_along_axis(spread, cos_idx, axis=1)
    sin = jnp.take_along_axis(spread, cos_idx + half, axis=1)
    return (jnp.where(lane < ROPE_DIMS, cos, 1.0),
            jnp.where((lane >= half) & (lane < ROPE_DIMS), sin, 0.0),
            jnp.where(lane < half, -sin, 0.0))


def _in_proj_kernel(h_ref, g_ref, w_hbm, cs_ref, cw_ref,
                    a_ref, b_ref, oc_ref, hn_ref, carry_ref, w_ref, w_stage, w_sem,
                    *, layer, tiles_per_seq):
    i = pl.program_id(0)

    @pl.when(i == 0)
    def _():
        _load_weights_bf16([(w_hbm, w_ref)], layer, w_stage, w_sem)

    hn_ref[...] = _rms_norm(h_ref[...], g_ref[...], NORM_EPS).astype(BF16)

    cos, sin_up, sin_dn = _rope_lane_tables(cs_ref[...])
    half = ROPE_DIMS // 2

    def proj(c0, width):
        return jnp.dot(hn_ref[...], w_ref[:, c0:c0 + width], preferred_element_type=F32)

    def rope(y):
        return (y * cos + pltpu.roll(y, half, axis=1) * sin_up
                + pltpu.roll(y, LANES - half, axis=1) * sin_dn)

    def rope_cols(c0, width, scale, out_ref, o0):
        y = proj(c0, width)
        for k in range(width // LANES):
            r = rope(y[:, k * LANES:(k + 1) * LANES])
            if scale != 1.0:
                r = r * scale
            out_ref[:, o0 + k * LANES:o0 + (k + 1) * LANES] = r.astype(BF16)

    scale = HEAD_DIM ** -0.5 * LOG2_E
    rope_cols(0, A_WIDTH, scale, a_ref, 0)
    rope_cols(A_WIDTH, A_WIDTH, 1.0, a_ref, A_WIDTH)
    a_ref[:, 2 * A_WIDTH:3 * A_WIDTH] = proj(2 * A_WIDTH, A_WIDTH).astype(BF16)
    b0 = 3 * A_WIDTH
    rope_cols(b0, B_QK, scale, b_ref, 0)
    rope_cols(b0 + B_QK, B_QK, 1.0, b_ref, B_QK)
    b_ref[:, 2 * B_QK:2 * B_QK + B_V] = proj(b0 + 2 * B_QK, B_V).astype(BF16)

    c0 = b0 + 2 * B_QK + B_V
    gate = proj(c0, C_WIDTH)
    prod = proj(c0 + C_WIDTH, C_WIDTH) * proj(c0 + 2 * C_WIDTH, C_WIDTH)

    @pl.when(i % tiles_per_seq == 0)
    def _():
        carry_ref[...] = jnp.zeros_like(carry_ref)

    prev = carry_ref[...]
    carry_ref[...] = prod[prod.shape[0] - CONV_HALO_ROWS:, :]
    oc_ref[...] = (gate * _causal_conv3(prod, prev, cw_ref[...])).astype(BF16)


WEIGHT_CHUNK = (256, 512)
WEIGHT_RING = 12


def _weight_ring_scratch():
    return [pltpu.VMEM((WEIGHT_RING,) + WEIGHT_CHUNK, F32), pltpu.SemaphoreType.DMA((WEIGHT_RING,))]


def _load_weights_bf16(jobs, layer, stage_ref, sem):
    cr, cc = WEIGHT_CHUNK
    chunks = [(w, dst, r0, c0) for w, dst in jobs
              for r0 in range(0, dst.shape[0], cr) for c0 in range(0, dst.shape[1], cc)]

    def copy(k):
        w, _, r0, c0 = chunks[k]
        slot = k % WEIGHT_RING
        return pltpu.make_async_copy(w.at[layer, r0:r0 + cr, c0:c0 + cc], stage_ref.at[slot],
                                     sem.at[slot])

    for k in range(min(WEIGHT_RING, len(chunks))):
        copy(k).start()
    for k, (_, dst, r0, c0) in enumerate(chunks):
        copy(k).wait()
        dst[r0:r0 + cr, c0:c0 + cc] = stage_ref[k % WEIGHT_RING].astype(BF16)
        if k + WEIGHT_RING < len(chunks):
            copy(k + WEIGHT_RING).start()


def _in_proj(h, g, w, layer, rope_cs, conv_w, seq):
    t = h.shape[0]
    rows = IN_ROWS
    row_spec = lambda width: pl.BlockSpec((rows, width), lambda i: (i, 0))
    full = lambda shape: pl.BlockSpec(shape, lambda i: (0, 0))
    return pl.pallas_call(
        functools.partial(_in_proj_kernel, layer=layer, tiles_per_seq=seq // rows),
        grid=(t // rows,),
        in_specs=[row_spec(D_MODEL), full((1, D_MODEL)), pl.BlockSpec(memory_space=pl.ANY),
                  pl.BlockSpec((rows // ROPE_TOKENS_PER_ROW, LANES), lambda i: (i, 0)),
                  full((CONV_WIDTH, C_WIDTH))],
        out_specs=[row_spec(3 * A_WIDTH), row_spec(2 * B_QK + B_V), row_spec(C_WIDTH)],
        out_shape=[jax.ShapeDtypeStruct((t, 3 * A_WIDTH), BF16),
                   jax.ShapeDtypeStruct((t, 2 * B_QK + B_V), BF16),
                   jax.ShapeDtypeStruct((t, C_WIDTH), BF16)],
        scratch_shapes=[pltpu.VMEM((rows, D_MODEL), BF16),
                        pltpu.VMEM((CONV_HALO_ROWS, C_WIDTH), F32),
                        pltpu.VMEM(w.shape[1:], BF16)] + _weight_ring_scratch(),
        compiler_params=pltpu.CompilerParams(dimension_semantics=("arbitrary",),
                                             vmem_limit_bytes=VMEM_LIMIT_BYTES),
        name="in_proj",
    )(h, g, w, rope_cs, conv_w)


ATT_ROWS = 512
ATT_GROUPS = 2
SCORE_SLOTS = 2
ONES_ROWS = 16


def _dilated_bias_table(seq):
    u = np.arange(seq)[:, None]
    r = np.arange(ATT_ROWS)[None, :]
    delta = r + seq - ATT_ROWS - u
    count = np.zeros(delta.shape, np.int64)
    for window, dil in DILATED_PATTERNS:
        count += (delta >= 0) & (delta <= window) & (delta % dil == 0)
    return np.where(count > 0, np.log2(np.maximum(count, 1)), MASK_BIAS).astype(np.float32)


def _causal_bias_table():
    key = np.arange(ATT_ROWS)[:, None]
    query = np.arange(ATT_ROWS)[None, :]
    return np.where(key <= query, 0.0, MASK_BIAS).astype(np.float32)


def _pair_attention(q_ref, k_ref, v_ref, vt_ref, s_refs, block_bias, emit):
    rows = ATT_ROWS
    n_chunks = q_ref.shape[0] // rows
    vt_ref[0:LANES, :] = v_ref[...].astype(F32).T.astype(BF16)
    vt_ref[LANES:, :] = jnp.ones((ONES_ROWS, vt_ref.shape[1]), BF16)
    lane = lax.broadcasted_iota(jnp.int32, (rows, LANES), 1)

    def stacked_queries(c):
        q = q_ref[c * rows:(c + 1) * rows, :]
        zero = jnp.zeros_like(q)
        return jnp.concatenate([jnp.where(lane < HEAD_DIM, q, zero),
                                jnp.where(lane >= HEAD_DIM, q, zero)], axis=0)

    half = rows // 2
    nt_dims = (((1,), (1,)), ((), ()))

    def late_lanes(x):
        return jnp.concatenate([x[:, half:rows], x[:, rows + half:]], axis=1)

    def late_lanes_only(x, fill):
        pad = jnp.full((x.shape[0], half), fill, x.dtype)
        return jnp.concatenate([pad, x[:, :half], pad, x[:, half:]], axis=1)

    def sublane_max(s):
        return jnp.max(s.reshape(s.shape[0] // 8, 8, s.shape[1]), axis=0)

    def score_block(c, j, qs, col_max):
        slot = s_refs[c % SCORE_SLOTS]
        bias = block_bias(c, j)
        if j < c:
            s = lax.dot_general(k_ref[j * rows:(j + 1) * rows, :], qs, nt_dims,
                                preferred_element_type=F32)
            if bias is not None:
                s = s + jnp.concatenate([bias, bias], axis=1)
            slot[j * rows:(j + 1) * rows, :] = s
            blk_max = sublane_max(s)
        else:
            mid = j * rows + half
            s = lax.dot_general(k_ref[j * rows:mid, :], qs, nt_dims, preferred_element_type=F32)
            s = s + jnp.concatenate([bias[:half], bias[:half]], axis=1)
            slot[j * rows:mid, :] = s
            qs_late = jnp.concatenate([qs[half:rows], qs[rows + half:]], axis=0)
            s_late = lax.dot_general(k_ref[mid:mid + half, :], qs_late, nt_dims,
                                     preferred_element_type=F32)
            s_late = s_late + jnp.concatenate([bias[half:, half:], bias[half:, half:]], axis=1)
            slot[mid:mid + half, 0:rows] = s_late
            blk_max = jnp.maximum(sublane_max(s), late_lanes_only(sublane_max(s_late), MASK_BIAS))
        return blk_max if col_max is None else jnp.maximum(col_max, blk_max)

    def value_block(c, j, m, acc):
        slot = s_refs[c % SCORE_SLOTS]
        if j < c:
            p = jnp.exp2(slot[j * rows:(j + 1) * rows, :] - m).astype(BF16)
            part = jnp.dot(vt_ref[:, j * rows:(j + 1) * rows], p, preferred_element_type=F32)
            return part if acc is None else acc + part
        mid = j * rows + half
        p = jnp.exp2(slot[j * rows:mid, :] - m).astype(BF16)
        part = jnp.dot(vt_ref[:, j * rows:mid], p, preferred_element_type=F32)
        acc = part if acc is None else acc + part
        p_late = jnp.exp2(slot[mid:mid + half, 0:rows] - late_lanes(m)).astype(BF16)
        part = jnp.dot(vt_ref[:, mid:mid + half], p_late, preferred_element_type=F32)
        return acc + late_lanes_only(part, 0.0)

    def score_pass(c):
        qs = stacked_queries(c)
        col_max = None
        for j in range(c + 1):
            col_max = score_block(c, j, qs, col_max)
        return col_max

    col_max = score_pass(0)
    for c in range(n_chunks):
        m = jnp.max(col_max, axis=0, keepdims=True)
        if c + 1 < n_chunks:
            col_max = score_pass(c + 1)
        acc = None
        for j in range(c + 1):
            acc = value_block(c, j, m, acc)
        emit(c, acc[0:LANES], acc[LANES:LANES + 1])


def _dilated_attn_kernel(q_ref, k_ref, v_ref, bias_ref, o_ref, vt_ref, *s_refs):
    rows = ATT_ROWS
    seq = q_ref.shape[0]

    def block_bias(c, j):
        first = seq - (c + 1 - j) * rows
        return bias_ref[first:first + rows, :]

    for g in range(q_ref.shape[1] // LANES):
        lanes = pl.ds(g * LANES, LANES)
        out = o_ref.at[:, lanes]

        def emit(c, o_t, l, out=out):
            o_t = o_t * (1.0 / l)
            o = jnp.concatenate([o_t[:HEAD_DIM, :rows], o_t[HEAD_DIM:, rows:]], axis=0)
            out[c * rows:(c + 1) * rows, :] = o.T.astype(out.dtype)

        _pair_attention(q_ref.at[:, lanes], k_ref.at[:, lanes], v_ref.at[:, lanes], vt_ref, s_refs,
                        block_bias, emit)


def _diff_attn_kernel(q_ref, k_ref, v_ref, bias_ref, lq1_ref, lk1_ref, lq2_ref, lk2_ref, g_ref,
                      o_ref, vt_ref, *s_refs, lam_init):
    rows = ATT_ROWS
    lam = (jnp.exp(jnp.sum(lq1_ref[...] * lk1_ref[...], axis=1, keepdims=True))
           - jnp.exp(jnp.sum(lq2_ref[...] * lk2_ref[...], axis=1, keepdims=True)) + lam_init)
    gain = g_ref[...]

    def block_bias(c, j):
        return bias_ref[...] if j == c else None

    for g in range(q_ref.shape[1] // LANES):
        lanes = pl.ds(g * LANES, LANES)
        out = o_ref.at[:, lanes]

        def emit(c, o_t, l, out=out):
            o_t = o_t * (1.0 / l)
            o = o_t[:, :rows] - lam * o_t[:, rows:]
            o = o * lax.rsqrt(jnp.mean(o * o, axis=0, keepdims=True) + SUBLN_EPS) * gain
            out[c * rows:(c + 1) * rows, :] = (o * (1.0 - lam_init)).T.astype(out.dtype)

        _pair_attention(q_ref.at[:, lanes], k_ref.at[:, lanes], v_ref.at[:, lanes], vt_ref, s_refs,
                        block_bias, emit)


def _attention_call(body, qkv, bias, extra, seq, n_pairs, q_off, k_off, v_off, name):
    t = qkv.shape[0]
    batch = t // seq
    small = lambda a: pl.BlockSpec(a.shape, lambda b, p: (0,) * a.ndim)
    width = ATT_GROUPS * LANES
    col = lambda off: pl.BlockSpec((seq, width), lambda b, p: (b, off // ATT_GROUPS + p))
    return pl.pallas_call(
        body,
        grid=(batch, n_pairs // ATT_GROUPS),
        in_specs=[col(q_off), col(k_off), col(v_off), small(bias)] + [small(a) for a in extra],
        out_specs=col(0),
        out_shape=jax.ShapeDtypeStruct((t, n_pairs * LANES), BF16),
        scratch_shapes=([pltpu.VMEM((LANES + ONES_ROWS, seq), BF16)]
                        + [pltpu.VMEM((seq, 2 * ATT_ROWS), F32)] * SCORE_SLOTS),
        compiler_params=pltpu.CompilerParams(dimension_semantics=("arbitrary", "arbitrary"),
                                             vmem_limit_bytes=VMEM_LIMIT_BYTES),
        name=name,
    )(qkv, qkv, qkv, bias, *extra)


FFN_ROWS = 512
FFN_CHUNK = 1024


def _mix_ffn_kernel(h_ref, oa_ref, ob_ref, oc_ref, wo_hbm, g_ref, wup_hbm, cw_ref, wd_hbm, fg_ref,
                    o_ref, hn_ref, act_ref, carry_ref, wo_ref, wup_ref, wd_ref, w_stage, w_sem,
                    *, layer, tiles_per_seq, final_norm):
    i = pl.program_id(0)

    @pl.when(i == 0)
    def _():
        _load_weights_bf16([(wo_hbm, wo_ref), (wup_hbm, wup_ref), (wd_hbm, wd_ref)], layer,
                           w_stage, w_sem)

    @pl.when(i % tiles_per_seq == 0)
    def _():
        carry_ref[...] = jnp.zeros_like(carry_ref)

    mixed = jnp.concatenate([oa_ref[...], ob_ref[...], oc_ref[...]], axis=1)
    h_mid = h_ref[...] + jnp.dot(mixed, wo_ref[...], preferred_element_type=F32)
    o_ref[...] = h_mid
    hn_ref[...] = _rms_norm(h_mid, g_ref[...], NORM_EPS).astype(BF16)

    def gate_up(cols):
        up_cols = slice(D_FF + cols.start, D_FF + cols.stop)
        hn = hn_ref[...]
        return (jnp.dot(hn, wup_ref[:, cols], preferred_element_type=F32),
                jnp.dot(hn, wup_ref[:, up_cols], preferred_element_type=F32))

    def activate(cols, gate, up):
        prev = carry_ref[:, cols]
        carry_ref[:, cols] = gate[gate.shape[0] - CONV_HALO_ROWS:, :]
        gc = _causal_conv3(gate, prev, cw_ref[:, cols])
        act_ref[:, cols] = (gc / (1.0 + jnp.exp(-gc)) * up).astype(BF16)

    chunks = [slice(c0, min(c0 + FFN_CHUNK, D_FF)) for c0 in range(0, D_FF, FFN_CHUNK)]
    pending = gate_up(chunks[0])
    for c, cols in enumerate(chunks):
        nxt = gate_up(chunks[c + 1]) if c + 1 < len(chunks) else None
        activate(cols, *pending)
        pending = nxt

    out = o_ref[...] + jnp.dot(act_ref[...], wd_ref[...], preferred_element_type=F32)
    if final_norm:
        out = _rms_norm(out, fg_ref[...], NORM_EPS)
    o_ref[...] = out


def _mix_ffn(h, o_a, o_b, o_c, w_out, g, w_up, conv_w, w_down, final_g, layer, seq, final_norm):
    t = h.shape[0]
    rows = FFN_ROWS
    row_spec = lambda width: pl.BlockSpec((rows, width), lambda i: (i, 0))
    full = lambda shape: pl.BlockSpec(shape, lambda i: (0, 0))
    hbm = pl.BlockSpec(memory_space=pl.ANY)
    return pl.pallas_call(
        functools.partial(_mix_ffn_kernel, layer=layer, tiles_per_seq=seq // rows,
                          final_norm=final_norm),
        grid=(t // rows,),
        in_specs=[row_spec(D_MODEL), row_spec(A_WIDTH), row_spec(B_V), row_spec(C_WIDTH),
                  hbm, full((1, D_MODEL)), hbm, full((CONV_WIDTH, D_FF)), hbm, full((1, D_MODEL))],
        out_specs=row_spec(D_MODEL),
        out_shape=jax.ShapeDtypeStruct((t, D_MODEL), F32),
        scratch_shapes=([pltpu.VMEM((rows, D_MODEL), BF16),
                         pltpu.VMEM((rows, D_FF), BF16),
                         pltpu.VMEM((CONV_HALO_ROWS, D_FF), F32)]
                        + [pltpu.VMEM(w.shape[1:], BF16) for w in (w_out, w_up, w_down)]
                        + _weight_ring_scratch()),
        compiler_params=pltpu.CompilerParams(dimension_semantics=("arbitrary",),
                                             vmem_limit_bytes=VMEM_LIMIT_BYTES),
        name="mix_ffn",
    )(h, o_a, o_b, o_c, w_out, g, w_up, conv_w, w_down, final_g)


def kernel(x, positions, norm_mix_g, w_in, lambda_q1, lambda_k1, lambda_q2, lambda_k2, subln_g,
           conv_mix_w, w_out, norm_ffn_g, w_up, conv_ffn_w, w_down, final_g):
    batch, seq, d = x.shape
    depth = w_in.shape[0]
    assert d == D_MODEL and seq % ATT_ROWS == 0 and seq % IN_ROWS == 0 and seq % FFN_ROWS == 0
    t = batch * seq
    h = x.reshape(t, d)
    tables = _rope_tables(positions.reshape(t))
    bias_a = jnp.asarray(_dilated_bias_table(seq))
    bias_b = jnp.asarray(_causal_bias_table())
    row = lambda v: v.reshape(1, -1)

    for layer in range(depth):
        a_qkv, b_qkv, o_c = _in_proj(h, row(norm_mix_g[layer]), w_in, layer, tables,
                                     conv_mix_w[layer], seq)
        o_a = _attention_call(_dilated_attn_kernel, a_qkv, bias_a, [], seq,
                              n_pairs=A_HEADS // 2, q_off=0, k_off=2, v_off=4, name="dilated_attn")
        lam_init = 0.8 - 0.6 * math.exp(-0.3 * layer)
        o_b = _attention_call(
            functools.partial(_diff_attn_kernel, lam_init=lam_init), b_qkv, bias_b,
            [row(lambda_q1[layer]), row(lambda_k1[layer]), row(lambda_q2[layer]),
             row(lambda_k2[layer]), subln_g[layer].reshape(-1, 1)],
            seq, n_pairs=B_HEADS, q_off=0, k_off=B_HEADS, v_off=2 * B_HEADS, name="diff_attn")
        h = _mix_ffn(h, o_a, o_b, o_c, w_out, row(norm_ffn_g[layer]), w_up, conv_ffn_w[layer], w_down,
                     row(final_g), layer, seq, final_norm=(layer == depth - 1))
    return h.reshape(batch, seq, d)
```

```python
import functools
import math

import numpy as np
import jax
import jax.numpy as jnp
from jax import lax
from jax.experimental import pallas as pl
from jax.experimental.pallas import tpu as pltpu

D_MODEL = 1024
HEAD_DIM = 64
A_HEADS = 4
A_WIDTH = 256
B_HEADS = 4
B_QK = 512
B_V = 512
C_WIDTH = 256
IN_COLS = 3072
DILATED_PATTERNS = ((128, 1), (512, 4), (2048, 16))
CONV_WIDTH = 3
D_FF = 2816
ROPE_THETA = 500000.0
ROPE_DIMS = HEAD_DIM // 4
NORM_EPS = 1e-6
SUBLN_EPS = 1e-5

LANES = 128
CONV_HALO_ROWS = 8
MASK_BIAS = -1e30
LOG2_E = math.log2(math.e)
VMEM_LIMIT_BYTES = 56 * 1024 * 1024

F32 = jnp.float32
BF16 = jnp.bfloat16


def _rms_norm(x, g, eps):
    y = x * lax.rsqrt(jnp.mean(x * x, axis=-1, keepdims=True) + eps)
    return y * g


def _causal_conv3(p, prev, w):
    rows = lax.broadcasted_iota(jnp.int32, p.shape, 0)
    last = prev[CONV_HALO_ROWS - 1:CONV_HALO_ROWS, :]
    last2 = prev[CONV_HALO_ROWS - 2:CONV_HALO_ROWS - 1, :]
    p1 = jnp.where(rows == 0, last, pltpu.roll(p, 1, axis=0))
    p2 = jnp.where(rows == 0, last2, jnp.where(rows == 1, last, pltpu.roll(p, 2, axis=0)))
    return w[0:1, :] * p2 + w[1:2, :] * p1 + w[2:3, :] * p


ROPE_TOKENS_PER_ROW = LANES // ROPE_DIMS


def _rope_table_kernel(pos_ref, inv_ref, o_ref):
    ang = pos_ref[...].astype(F32) * inv_ref[...]
    lane = lax.broadcasted_iota(jnp.int32, ang.shape, 1) % ROPE_DIMS
    o_ref[...] = jnp.where(lane < ROPE_DIMS // 2, jnp.cos(ang), jnp.sin(ang))


def _rope_tables(pos):
    t = pos.shape[0]
    half = ROPE_DIMS // 2
    inv = ROPE_THETA ** (-(jnp.arange(half, dtype=F32) * 2.0 / ROPE_DIMS))
    inv_lane = jnp.tile(inv, LANES // half)[None, :]
    rows = t // ROPE_TOKENS_PER_ROW
    pos_rep = jnp.repeat(pos.reshape(rows, ROPE_TOKENS_PER_ROW), ROPE_DIMS, axis=1)
    compact = pl.pallas_call(
        _rope_table_kernel,
        grid=(1,),
        in_specs=[pl.BlockSpec((rows, LANES), lambda i: (0, 0)),
                  pl.BlockSpec((1, LANES), lambda i: (0, 0))],
        out_specs=pl.BlockSpec((rows, LANES), lambda i: (0, 0)),
        out_shape=jax.ShapeDtypeStruct((rows, LANES), F32),
        name="rope_tables",
    )(pos_rep, inv_lane)
    return compact


IN_ROWS = 1024


def _rope_lane_tables(cs):
    half = ROPE_DIMS // 2
    groups = cs.shape[0]
    shape = (groups * ROPE_TOKENS_PER_ROW, LANES)
    spread = jnp.broadcast_to(cs[:, None, :], (groups, ROPE_TOKENS_PER_ROW, LANES)).reshape(shape)
    token = lax.broadcasted_iota(jnp.int32, shape, 0) % ROPE_TOKENS_PER_ROW
    lane = lax.broadcasted_iota(jnp.int32, shape, 1) % HEAD_DIM
    cos_idx = token * ROPE_DIMS + lane % half
    cos = jnp.take_along_axis(spread, cos_idx, axis=1)
    sin = jnp.take_along_axis(spread, cos_idx + half, axis=1)
    return (jnp.where(lane < ROPE_DIMS, cos, 1.0),
            jnp.where((lane >= half) & (lane < ROPE_DIMS), sin, 0.0),
            jnp.where(lane < half, -sin, 0.0))


def _in_proj_kernel(h_ref, g_ref, w_hbm, cs_ref, cw_ref,
                    a_ref, b_ref, oc_ref, hn_ref, carry_ref, w_ref, w_stage, w_sem,
                    *, layer, tiles_per_seq):
    i = pl.program_id(0)

    @pl.when(i == 0)
    def _():
        _load_weights_bf16([(w_hbm, w_ref)], layer, w_stage, w_sem)

    hn_ref[...] = _rms_norm(h_ref[...], g_ref[...], NORM_EPS).astype(BF16)

    cos, sin_up, sin_dn = _rope_lane_tables(cs_ref[...])
    half = ROPE_DIMS // 2

    def proj(c0, width):
        return jnp.dot(hn_ref[...], w_ref[:, c0:c0 + width], preferred_element_type=F32)

    def rope(y):
        return (y * cos + pltpu.roll(y, half, axis=1) * sin_up
                + pltpu.roll(y, LANES - half, axis=1) * sin_dn)

    def rope_cols(c0, width, scale, out_ref, o0):
        y = proj(c0, width)
        for k in range(width // LANES):
            r = rope(y[:, k * LANES:(k + 1) * LANES])
            if scale != 1.0:
                r = r * scale
            out_ref[:, o0 + k * LANES:o0 + (k + 1) * LANES] = r.astype(BF16)

    scale = HEAD_DIM ** -0.5 * LOG2_E
    rope_cols(0, A_WIDTH, scale, a_ref, 0)
    rope_cols(A_WIDTH, A_WIDTH, 1.0, a_ref, A_WIDTH)
    a_ref[:, 2 * A_WIDTH:3 * A_WIDTH] = proj(2 * A_WIDTH, A_WIDTH).astype(BF16)
    b0 = 3 * A_WIDTH
    rope_cols(b0, B_QK, scale, b_ref, 0)
    rope_cols(b0 + B_QK, B_QK, 1.0, b_ref, B_QK)
    b_ref[:, 2 * B_QK:2 * B_QK + B_V] = proj(b0 + 2 * B_QK, B_V).astype(BF16)

    c0 = b0 + 2 * B_QK + B_V
    gate = proj(c0, C_WIDTH)
    prod = proj(c0 + C_WIDTH, C_WIDTH) * proj(c0 + 2 * C_WIDTH, C_WIDTH)

    @pl.when(i % tiles_per_seq == 0)
    def _():
        carry_ref[...] = jnp.zeros_like(carry_ref)

    prev = carry_ref[...]
    carry_ref[...] = prod[prod.shape[0] - CONV_HALO_ROWS:, :]
    oc_ref[...] = (gate * _causal_conv3(prod, prev, cw_ref[...])).astype(BF16)


WEIGHT_CHUNK = (256, 512)
WEIGHT_RING = 12


def _weight_ring_scratch():
    return [pltpu.VMEM((WEIGHT_RING,) + WEIGHT_CHUNK, F32), pltpu.SemaphoreType.DMA((WEIGHT_RING,))]


def _load_weights_bf16(jobs, layer, stage_ref, sem):
    cr, cc = WEIGHT_CHUNK
    chunks = [(w, dst, r0, c0) for w, dst in jobs
              for r0 in range(0, dst.shape[0], cr) for c0 in range(0, dst.shape[1], cc)]

    def copy(k):
        w, _, r0, c0 = chunks[k]
        slot = k % WEIGHT_RING
        return pltpu.make_async_copy(w.at[layer, r0:r0 + cr, c0:c0 + cc], stage_ref.at[slot],
                                     sem.at[slot])

    for k in range(min(WEIGHT_RING, len(chunks))):
        copy(k).start()
    for k, (_, dst, r0, c0) in enumerate(chunks):
        copy(k).wait()
        dst[r0:r0 + cr, c0:c0 + cc] = stage_ref[k % WEIGHT_RING].astype(BF16)
        if k + WEIGHT_RING < len(chunks):
            copy(k + WEIGHT_RING).start()


def _in_proj(h, g, w, layer, rope_cs, conv_w, seq):
    t = h.shape[0]
    rows = IN_ROWS
    row_spec = lambda width: pl.BlockSpec((rows, width), lambda i: (i, 0))
    full = lambda shape: pl.BlockSpec(shape, lambda i: (0, 0))
    return pl.pallas_call(
        functools.partial(_in_proj_kernel, layer=layer, tiles_per_seq=seq // rows),
        grid=(t // rows,),
        in_specs=[row_spec(D_MODEL), full((1, D_MODEL)), pl.BlockSpec(memory_space=pl.ANY),
                  pl.BlockSpec((rows // ROPE_TOKENS_PER_ROW, LANES), lambda i: (i, 0)),
                  full((CONV_WIDTH, C_WIDTH))],
        out_specs=[row_spec(3 * A_WIDTH), row_spec(2 * B_QK + B_V), row_spec(C_WIDTH)],
        out_shape=[jax.ShapeDtypeStruct((t, 3 * A_WIDTH), BF16),
                   jax.ShapeDtypeStruct((t, 2 * B_QK + B_V), BF16),
                   jax.ShapeDtypeStruct((t, C_WIDTH), BF16)],
        scratch_shapes=[pltpu.VMEM((rows, D_MODEL), BF16),
                        pltpu.VMEM((CONV_HALO_ROWS, C_WIDTH), F32),
                        pltpu.VMEM(w.shape[1:], BF16)] + _weight_ring_scratch(),
        compiler_params=pltpu.CompilerParams(dimension_semantics=("arbitrary",),
                                             vmem_limit_bytes=VMEM_LIMIT_BYTES),
        name="in_proj",
    )(h, g, w, rope_cs, conv_w)


ATT_ROWS = 512
ATT_GROUPS = 2
SCORE_SLOTS = 2
ONES_ROWS = 16


def _dilated_bias_table(seq):
    u = np.arange(seq)[:, None]
    r = np.arange(ATT_ROWS)[None, :]
    delta = r + seq - ATT_ROWS - u
    count = np.zeros(delta.shape, np.int64)
    for window, dil in DILATED_PATTERNS:
        count += (delta >= 0) & (delta <= window) & (delta % dil == 0)
    return np.where(count > 0, np.log2(np.maximum(count, 1)), MASK_BIAS).astype(np.float32)


def _causal_bias_table():
    key = np.arange(ATT_ROWS)[:, None]
    query = np.arange(ATT_ROWS)[None, :]
    return np.where(key <= query, 0.0, MASK_BIAS).astype(np.float32)


def _pair_attention(q_ref, k_ref, v_ref, vt_ref, s_refs, block_bias, emit, mxu_denominators):
    rows = ATT_ROWS
    n_chunks = q_ref.shape[0] // rows
    vt_ref[0:LANES, :] = v_ref[...].astype(F32).T.astype(BF16)
    vt_ref[LANES:, :] = jnp.ones((ONES_ROWS, vt_ref.shape[1]), BF16)
    lane = lax.broadcasted_iota(jnp.int32, (rows, LANES), 1)

    def stacked_queries(c):
        q = q_ref[c * rows:(c + 1) * rows, :]
        zero = jnp.zeros_like(q)
        return jnp.concatenate([jnp.where(lane < HEAD_DIM, q, zero),
                                jnp.where(lane >= HEAD_DIM, q, zero)], axis=0)

    half = rows // 2
    nt_dims = (((1,), (1,)), ((), ()))

    def late_lanes(x):
        return jnp.concatenate([x[:, half:rows], x[:, rows + half:]], axis=1)

    def late_lanes_only(x, fill):
        pad = jnp.full((x.shape[0], half), fill, x.dtype)
        return jnp.concatenate([pad, x[:, :half], pad, x[:, half:]], axis=1)

    def sublane_max(s):
        return jnp.max(s.reshape(s.shape[0] // 8, 8, s.shape[1]), axis=0)

    def score_block(c, j, qs, col_max):
        slot = s_refs[c % SCORE_SLOTS]
        bias = block_bias(c, j)
        if j < c:
            s = lax.dot_general(k_ref[j * rows:(j + 1) * rows, :], qs, nt_dims,
                                preferred_element_type=F32)
            if bias is not None:
                s = s + jnp.concatenate([bias, bias], axis=1)
            slot[j * rows:(j + 1) * rows, :] = s
            blk_max = sublane_max(s)
        else:
            mid = j * rows + half
            s = lax.dot_general(k_ref[j * rows:mid, :], qs, nt_dims, preferred_element_type=F32)
            s = s + jnp.concatenate([bias[:half], bias[:half]], axis=1)
            slot[j * rows:mid, :] = s
            qs_late = jnp.concatenate([qs[half:rows], qs[rows + half:]], axis=0)
            s_late = lax.dot_general(k_ref[mid:mid + half, :], qs_late, nt_dims,
                                     preferred_element_type=F32)
            s_late = s_late + jnp.concatenate([bias[half:, half:], bias[half:, half:]], axis=1)
            slot[mid:mid + half, 0:rows] = s_late
            blk_max = jnp.maximum(sublane_max(s), late_lanes_only(sublane_max(s_late), MASK_BIAS))
        return blk_max if col_max is None else jnp.maximum(col_max, blk_max)

    v_rows = LANES + ONES_ROWS if mxu_denominators else LANES

    def weighted_values(p, k0, k1):
        part = jnp.dot(vt_ref[0:v_rows, k0:k1], p.astype(BF16), preferred_element_type=F32)
        if mxu_denominators:
            return part
        sums = jnp.sum(p.reshape(p.shape[0] // 8, 8, p.shape[1]), axis=0)
        return jnp.concatenate([part, sums], axis=0)

    def value_block(c, j, m, acc):
        slot = s_refs[c % SCORE_SLOTS]
        if j < c:
            part = weighted_values(jnp.exp2(slot[j * rows:(j + 1) * rows, :] - m),
                                   j * rows, (j + 1) * rows)
            return part if acc is None else acc + part
        mid = j * rows + half
        part = weighted_values(jnp.exp2(slot[j * rows:mid, :] - m), j * rows, mid)
        acc = part if acc is None else acc + part
        part = weighted_values(jnp.exp2(slot[mid:mid + half, 0:rows] - late_lanes(m)), mid, mid + half)
        return acc + late_lanes_only(part, 0.0)

    def score_pass(c):
        qs = stacked_queries(c)
        col_max = None
        for j in range(c + 1):
            col_max = score_block(c, j, qs, col_max)
        return col_max

    col_max = score_pass(0)
    for c in range(n_chunks):
        m = jnp.max(col_max, axis=0, keepdims=True)
        if c + 1 < n_chunks:
            col_max = score_pass(c + 1)
        acc = None
        for j in range(c + 1):
            acc = value_block(c, j, m, acc)
        den = acc[LANES:LANES + 1] if mxu_denominators else jnp.sum(acc[LANES:], axis=0, keepdims=True)
        emit(c, acc[0:LANES], den)


def _dilated_attn_kernel(q_ref, k_ref, v_ref, bias_ref, o_ref, vt_ref, *s_refs):
    rows = ATT_ROWS
    seq = q_ref.shape[0]

    def block_bias(c, j):
        first = seq - (c + 1 - j) * rows
        return bias_ref[first:first + rows, :]

    for g in range(q_ref.shape[1] // LANES):
        lanes = pl.ds(g * LANES, LANES)
        out = o_ref.at[:, lanes]

        def emit(c, o_t, l, out=out):
            o_t = o_t * (1.0 / l)
            o = jnp.concatenate([o_t[:HEAD_DIM, :rows], o_t[HEAD_DIM:, rows:]], axis=0)
            out[c * rows:(c + 1) * rows, :] = o.T.astype(out.dtype)

        _pair_attention(q_ref.at[:, lanes], k_ref.at[:, lanes], v_ref.at[:, lanes], vt_ref, s_refs,
                        block_bias, emit, mxu_denominators=True)


def _diff_attn_kernel(q_ref, k_ref, v_ref, bias_ref, lq1_ref, lk1_ref, lq2_ref, lk2_ref, g_ref,
                      o_ref, vt_ref, *s_refs, lam_init):
    rows = ATT_ROWS
    lam = (jnp.exp(jnp.sum(lq1_ref[...] * lk1_ref[...], axis=1, keepdims=True))
           - jnp.exp(jnp.sum(lq2_ref[...] * lk2_ref[...], axis=1, keepdims=True)) + lam_init)
    gain = g_ref[...]

    def block_bias(c, j):
        return bias_ref[...] if j == c else None

    for g in range(q_ref.shape[1] // LANES):
        lanes = pl.ds(g * LANES, LANES)
        out = o_ref.at[:, lanes]

        def emit(c, o_t, l, out=out):
            o_t = o_t * (1.0 / l)
            o = o_t[:, :rows] - lam * o_t[:, rows:]
            o = o * lax.rsqrt(jnp.mean(o * o, axis=0, keepdims=True) + SUBLN_EPS) * gain
            out[c * rows:(c + 1) * rows, :] = (o * (1.0 - lam_init)).T.astype(out.dtype)

        _pair_attention(q_ref.at[:, lanes], k_ref.at[:, lanes], v_ref.at[:, lanes], vt_ref, s_refs,
                        block_bias, emit, mxu_denominators=False)


def _attention_call(body, qkv, bias, extra, seq, n_pairs, q_off, k_off, v_off, name):
    t = qkv.shape[0]
    batch = t // seq
    small = lambda a: pl.BlockSpec(a.shape, lambda b, p: (0,) * a.ndim)
    width = ATT_GROUPS * LANES
    col = lambda off: pl.BlockSpec((seq, width), lambda b, p: (b, off // ATT_GROUPS + p))
    return pl.pallas_call(
        body,
        grid=(batch, n_pairs // ATT_GROUPS),
        in_specs=[col(q_off), col(k_off), col(v_off), small(bias)] + [small(a) for a in extra],
        out_specs=col(0),
        out_shape=jax.ShapeDtypeStruct((t, n_pairs * LANES), BF16),
        scratch_shapes=([pltpu.VMEM((LANES + ONES_ROWS, seq), BF16)]
                        + [pltpu.VMEM((seq, 2 * ATT_ROWS), F32)] * SCORE_SLOTS),
        compiler_params=pltpu.CompilerParams(dimension_semantics=("arbitrary", "arbitrary"),
                                             vmem_limit_bytes=VMEM_LIMIT_BYTES),
        name=name,
    )(qkv, qkv, qkv, bias, *extra)


FFN_ROWS = 512
FFN_CHUNK = 1024


def _mix_ffn_kernel(h_ref, oa_ref, ob_ref, oc_ref, wo_hbm, g_ref, wup_hbm, cw_ref, wd_hbm, fg_ref,
                    o_ref, hn_ref, act_ref, carry_ref, wo_ref, wup_ref, wd_ref, w_stage, w_sem,
                    *, layer, tiles_per_seq, final_norm):
    i = pl.program_id(0)

    @pl.when(i == 0)
    def _():
        _load_weights_bf16([(wo_hbm, wo_ref), (wup_hbm, wup_ref), (wd_hbm, wd_ref)], layer,
                           w_stage, w_sem)

    @pl.when(i % tiles_per_seq == 0)
    def _():
        carry_ref[...] = jnp.zeros_like(carry_ref)

    mixed = jnp.concatenate([oa_ref[...], ob_ref[...], oc_ref[...]], axis=1)
    h_mid = h_ref[...] + jnp.dot(mixed, wo_ref[...], preferred_element_type=F32)
    o_ref[...] = h_mid
    hn_ref[...] = _rms_norm(h_mid, g_ref[...], NORM_EPS).astype(BF16)

    def gate_up(cols):
        up_cols = slice(D_FF + cols.start, D_FF + cols.stop)
        hn = hn_ref[...]
        return (jnp.dot(hn, wup_ref[:, cols], preferred_element_type=F32),
                jnp.dot(hn, wup_ref[:, up_cols], preferred_element_type=F32))

    def activate(cols, gate, up):
        prev = carry_ref[:, cols]
        carry_ref[:, cols] = gate[gate.shape[0] - CONV_HALO_ROWS:, :]
        gc = _causal_conv3(gate, prev, cw_ref[:, cols])
        act_ref[:, cols] = (gc / (1.0 + jnp.exp(-gc)) * up).astype(BF16)

    chunks = [slice(c0, min(c0 + FFN_CHUNK, D_FF)) for c0 in range(0, D_FF, FFN_CHUNK)]
    pending = gate_up(chunks[0])
    for c, cols in enumerate(chunks):
        nxt = gate_up(chunks[c + 1]) if c + 1 < len(chunks) else None
        activate(cols, *pending)
        pending = nxt

    out = o_ref[...] + jnp.dot(act_ref[...], wd_ref[...], preferred_element_type=F32)
    if final_norm:
        out = _rms_norm(out, fg_ref[...], NORM_EPS)
    o_ref[...] = out


def _mix_ffn(h, o_a, o_b, o_c, w_out, g, w_up, conv_w, w_down, final_g, layer, seq, final_norm):
    t = h.shape[0]
    rows = FFN_ROWS
    row_spec = lambda width: pl.BlockSpec((rows, width), lambda i: (i, 0))
    full = lambda shape: pl.BlockSpec(shape, lambda i: (0, 0))
    hbm = pl.BlockSpec(memory_space=pl.ANY)
    return pl.pallas_call(
        functools.partial(_mix_ffn_kernel, layer=layer, tiles_per_seq=seq // rows,
                          final_norm=final_norm),
        grid=(t // rows,),
        in_specs=[row_spec(D_MODEL), row_spec(A_WIDTH), row_spec(B_V), row_spec(C_WIDTH),
                  hbm, full((1, D_MODEL)), hbm, full((CONV_WIDTH, D_FF)), hbm, full((1, D_MODEL))],
        out_specs=row_spec(D_MODEL),
        out_shape=jax.ShapeDtypeStruct((t, D_MODEL), F32),
        scratch_shapes=([pltpu.VMEM((rows, D_MODEL), BF16),
                         pltpu.VMEM((rows, D_FF), BF16),
                         pltpu.VMEM((CONV_HALO_ROWS, D_FF), F32)]
                        + [pltpu.VMEM(w.shape[1:], BF16) for w in (w_out, w_up, w_down)]
                        + _weight_ring_scratch()),
        compiler_params=pltpu.CompilerParams(dimension_semantics=("arbitrary",),
                                             vmem_limit_bytes=VMEM_LIMIT_BYTES),
        name="mix_ffn",
    )(h, o_a, o_b, o_c, w_out, g, w_up, conv_w, w_down, final_g)


def kernel(x, positions, norm_mix_g, w_in, lambda_q1, lambda_k1, lambda_q2, lambda_k2, subln_g,
           conv_mix_w, w_out, norm_ffn_g, w_up, conv_ffn_w, w_down, final_g):
    batch, seq, d = x.shape
    depth = w_in.shape[0]
    assert d == D_MODEL and seq % ATT_ROWS == 0 and seq % IN_ROWS == 0 and seq % FFN_ROWS == 0
    t = batch * seq
    h = x.reshape(t, d)
    tables = _rope_tables(positions.reshape(t))
    bias_a = jnp.asarray(_dilated_bias_table(seq))
    bias_b = jnp.asarray(_causal_bias_table())
    row = lambda v: v.reshape(1, -1)

    for layer in range(depth):
        a_qkv, b_qkv, o_c = _in_proj(h, row(norm_mix_g[layer]), w_in, layer, tables,
                                     conv_mix_w[layer], seq)
        o_a = _attention_call(_dilated_attn_kernel, a_qkv, bias_a, [], seq,
                              n_pairs=A_HEADS // 2, q_off=0, k_off=2, v_off=4, name="dilated_attn")
        lam_init = 0.8 - 0.6 * math.exp(-0.3 * layer)
        o_b = _attention_call(
            functools.partial(_diff_attn_kernel, lam_init=lam_init), b_qkv, bias_b,
            [row(lambda_q1[layer]), row(lambda_k1[layer]), row(lambda_q2[layer]),
             row(lambda_k2[layer]), subln_g[layer].reshape(-1, 1)],
            seq, n_pairs=B_HEADS, q_off=0, k_off=B_HEADS, v_off=2 * B_HEADS, name="diff_attn")
        h = _mix_ffn(h, o_a, o_b, o_c, w_out, row(norm_ffn_g[layer]), w_up, conv_ffn_w[layer], w_down,
                     row(final_g), layer, seq, final_norm=(layer == depth - 1))
    return h.reshape(batch, seq, d)
```
